```python
import math
import jax, jax.numpy as jnp
from jax import lax
import numpy as np

D_MODEL = 4096
BATCH = 2
SEQ = 4096
DEPTH = 4

HEAD_DIM = 128
ROPE_THETA = 10000.0
LN_EPS = 1e-5
RMS_EPS = 1e-6
ATT_HEADS = 12
ATT_KV_HEADS = 4
IDX_HEADS = 16
IDX_DIM = 64
TOPK_MAX = 256
Q_BLOCK = 128
SSM_D_INNER = 1536
SSM_HEAD_DIM = 64
SSM_HEADS = SSM_D_INNER // SSM_HEAD_DIM
SSM_GROUPS = 4
SSM_STATE = 128
SSM_CONV = 4
SSM_CHUNK = 128
SSM_CONV_DIM = SSM_D_INNER + 2 * SSM_GROUPS * SSM_STATE
GDN_HEADS = 12
GDN_DK = 128
GDN_DV = 128
GDN_CONV = 4
GDN_CHUNK = 64
GDN_CONV_DIM = GDN_HEADS * (2 * GDN_DK + GDN_DV)
N_BRANCH = 3
BRANCH_WIDTH = 1536
GATE_RANK = 512
D_FF = -(-8 * D_MODEL // (3 * 256)) * 256
DEEPNORM_ALPHA = (2.0 * DEPTH) ** 0.25
DEEPNORM_BETA = (8.0 * DEPTH) ** -0.25

IN_SPLITS = (
    ATT_HEADS * HEAD_DIM,
    ATT_KV_HEADS * HEAD_DIM,
    ATT_KV_HEADS * HEAD_DIM,
    IDX_HEADS * IDX_DIM,
    IDX_DIM,
    IDX_HEADS,
    SSM_D_INNER,
    SSM_CONV_DIM,
    SSM_HEADS,
    GDN_CONV_DIM,
    GDN_HEADS * GDN_DV,
    GDN_HEADS,
    GDN_HEADS,
    GATE_RANK,
)
IN_WIDTH = sum(IN_SPLITS)

kernel_name = 'hybrid_dsa_ssd_gdn_deepnorm'


def _in_offsets():
    return [int(o) for o in np.cumsum(IN_SPLITS)[:-1]]


def layer_norm(x, g, b):
    xf = x.astype(jnp.float32)
    mu = jnp.mean(xf, -1, keepdims=True)
    var = jnp.mean(jnp.square(xf - mu), -1, keepdims=True)
    return ((xf - mu) * lax.rsqrt(var + LN_EPS) * g + b).astype(x.dtype)


def rope_tables(positions, dim):
    inv = ROPE_THETA ** (-jnp.arange(0, dim, 2, dtype=jnp.float32) / dim)
    ang = positions.astype(jnp.float32)[..., None] * inv
    return jnp.cos(ang), jnp.sin(ang)


def apply_rope(x, cos, sin):
    x1, x2 = jnp.split(x.astype(jnp.float32), 2, axis=-1)
    c, s = cos[:, :, None], sin[:, :, None]
    return jnp.concatenate([x1 * c - x2 * s, x2 * c + x1 * s], -1).astype(x.dtype)


def causal_dwconv(x, w, b=None):
    K, C = w.shape
    y = lax.conv_general_dilated(x, w[:, None, :].astype(x.dtype), window_strides=(1,),
                                 padding=[(K - 1, 0)], dimension_numbers=('NWC', 'WIO', 'NWC'),
                                 feature_group_count=C)
    return y if b is None else y + b.astype(x.dtype)


def l2norm(x):
    return x * lax.rsqrt(jnp.sum(jnp.square(x), -1, keepdims=True) + RMS_EPS)


def sparse_indexed_attention(q, k, v, iq, ik, iw, cos, sin, cos_i, sin_i):
    f32 = jnp.float32
    B, S = q.shape[:2]
    grp = ATT_HEADS // ATT_KV_HEADS
    q = apply_rope(q.reshape(B, S, ATT_HEADS, HEAD_DIM), cos, sin)
    k = apply_rope(k.reshape(B, S, ATT_KV_HEADS, HEAD_DIM), cos, sin)
    v = v.reshape(B, S, ATT_KV_HEADS, HEAD_DIM)
    iq = apply_rope(iq.reshape(B, S, IDX_HEADS, IDX_DIM), cos_i, sin_i)
    ik = apply_rope(ik.reshape(B, S, 1, IDX_DIM), cos_i, sin_i)[:, :, 0].astype(f32)
    iw = iw.astype(f32) * IDX_HEADS ** -0.5
    topk = min(TOPK_MAX, S // 4)
    nb = S // Q_BLOCK
    key_pos = jnp.arange(S)

    def to_blocks(a):
        return jnp.moveaxis(a.reshape(B, nb, Q_BLOCK, *a.shape[2:]), 1, 0)

    def block(args):
        qb, iqb, iwb, start = args
        qpos = start + jnp.arange(Q_BLOCK)
        logits = jnp.einsum('bthd,bsd->bths', iqb.astype(f32), ik) * IDX_DIM ** -0.5
        score = jnp.einsum('bths,bth->bts', jax.nn.relu(logits), iwb)
        causal = key_pos[None, :] <= qpos[:, None]
        score = jnp.where(causal[None], score, -jnp.inf)
        _, idx = lax.top_k(score, topk)
        valid = idx <= qpos[None, :, None]
        ksel = jax.vmap(lambda kb, ib: kb[ib])(k, idx)
        vsel = jax.vmap(lambda vb, ib: vb[ib])(v, idx)
        qg = qb.reshape(B, Q_BLOCK, ATT_KV_HEADS, grp, HEAD_DIM)
        s = jnp.einsum('btgrd,btkgd->btgrk', qg, ksel, preferred_element_type=f32) * HEAD_DIM ** -0.5
        s = jnp.where(valid[:, :, None, None, :], s, -jnp.inf)
        p = jax.nn.softmax(s, axis=-1)
        o = jnp.einsum('btgrk,btkgd->btgrd', p.astype(v.dtype), vsel)
        return o.reshape(B, Q_BLOCK, ATT_HEADS * HEAD_DIM)

    starts = jnp.arange(nb, dtype=jnp.int32) * Q_BLOCK
    out = lax.map(block, (to_blocks(q), to_blocks(iq), to_blocks(iw), starts))
    return jnp.moveaxis(out, 0, 1).reshape(B, S, ATT_HEADS * HEAD_DIM).astype(v.dtype)


def mamba2_ssd(z, xbc, dt_raw, conv_w, conv_b, A_log, dt_bias, D_skip, norm_w):
    f32 = jnp.float32
    Bsz, S, _ = z.shape
    G, R, P, N, Q = SSM_GROUPS, SSM_HEADS // SSM_GROUPS, SSM_HEAD_DIM, SSM_STATE, SSM_CHUNK
    nc = S // Q
    xbc = jax.nn.silu(causal_dwconv(xbc, conv_w, conv_b)).astype(f32)
    xs, Bm, Cm = jnp.split(xbc, [SSM_D_INNER, SSM_D_INNER + G * N], axis=-1)
    dt = jax.nn.softplus(dt_raw.astype(f32) + dt_bias.astype(f32))
    A = -jnp.exp(A_log.astype(f32))
    X = xs.reshape(Bsz, nc, Q, G, R, P)
    dtc = dt.reshape(Bsz, nc, Q, G, R)
    Xdt = X * dtc[..., None]
    Bc = Bm.reshape(Bsz, nc, Q, G, N)
    Cc = Cm.reshape(Bsz, nc, Q, G, N)
    a_cs = jnp.cumsum(dtc * A.reshape(G, R), axis=2)
    causal = jnp.tril(jnp.ones((Q, Q), dtype=bool))[:, :, None, None]
    seg = a_cs[:, :, :, None] - a_cs[:, :, None, :]
    Lmat = jnp.exp(jnp.where(causal, seg, -jnp.inf))
    CB = jnp.einsum('bclgn,bcsgn->bclsg', Cc, Bc)
    y_diag = jnp.einsum('bclsgr,bcsgrp->bclgrp', CB[..., None] * Lmat, Xdt)
    states = jnp.einsum('bclgn,bclgr,bclgrp->bcgrpn', Bc, jnp.exp(a_cs[:, :, -1:] - a_cs), Xdt)
    chunk_decay = jnp.exp(a_cs[:, :, -1])

    def carry_state(h, inp):
        st, dec = inp
        return h * dec[..., None, None] + st, h

    h0 = jnp.zeros((Bsz, G, R, P, N), f32)
    _, h_prev = lax.scan(carry_state, h0, (jnp.moveaxis(states, 1, 0), jnp.moveaxis(chunk_decay, 1, 0)))
    h_prev = jnp.moveaxis(h_prev, 0, 1)
    y_off = jnp.einsum('bclgn,bcgrpn,bclgr->bclgrp', Cc, h_prev, jnp.exp(a_cs))
    y = y_diag + y_off + X * D_skip.astype(f32).reshape(G, R, 1)
    y = y.reshape(Bsz, S, SSM_D_INNER) * jax.nn.silu(z.astype(f32))
    yg = y.reshape(Bsz, S, G, SSM_D_INNER // G)
    yg = yg * lax.rsqrt(jnp.mean(jnp.square(yg), -1, keepdims=True) + RMS_EPS)
    return (yg.reshape(Bsz, S, SSM_D_INNER) * norm_w).astype(z.dtype)


def gated_deltanet(qkv, z, a_raw, b_raw, conv_w, A_log, dt_bias, norm_w):
    f32 = jnp.float32
    Bsz, S, _ = qkv.shape
    H, DK, DV, C = GDN_HEADS, GDN_DK, GDN_DV, GDN_CHUNK
    nc = S // C
    qkv = jax.nn.silu(causal_dwconv(qkv, conv_w)).astype(f32)
    q, k, v = jnp.split(qkv, [H * DK, 2 * H * DK], axis=-1)
    q = l2norm(q.reshape(Bsz, S, H, DK)) * DK ** -0.5
    k = l2norm(k.reshape(Bsz, S, H, DK))
    v = v.reshape(Bsz, S, H, DV)
    beta = jax.nn.sigmoid(b_raw.astype(f32))
    g = -jnp.exp(A_log.astype(f32)) * jax.nn.softplus(a_raw.astype(f32) + dt_bias.astype(f32))

    def chunks(t):
        return jnp.moveaxis(t.reshape(Bsz, nc, C, H, *t.shape[3:]), 3, 1)

    qc, kc, vc, bc = chunks(q), chunks(k), chunks(v), chunks(beta)
    g_cs = jnp.cumsum(chunks(g), axis=-1)
    tril = jnp.tril(jnp.ones((C, C), dtype=bool))
    strict = jnp.tril(jnp.ones((C, C), dtype=bool), -1)
    decay = jnp.exp(jnp.where(tril, g_cs[..., :, None] - g_cs[..., None, :], -jnp.inf))
    kb = kc * bc[..., None]
    lower = jnp.where(strict, jnp.einsum('bhcid,bhcjd->bhcij', kb, kc) * decay, 0.0)
    eye = jnp.eye(C, dtype=f32)
    T = lax.linalg.triangular_solve(lower + eye, jnp.broadcast_to(eye, lower.shape),
                                    left_side=True, lower=True, unit_diagonal=True)
    u = jnp.einsum('bhcij,bhcje->bhcie', T, vc * bc[..., None])
    w = jnp.einsum('bhcij,bhcjd->bhcid', T, kb * jnp.exp(g_cs)[..., None])
    intra = jnp.einsum('bhcid,bhcjd->bhcij', qc, kc) * decay
    q_dec = qc * jnp.exp(g_cs)[..., None]
    k_dec = kc * jnp.exp(g_cs[..., -1:] - g_cs)[..., None]
    chunk_decay = jnp.exp(g_cs[..., -1])

    def step(state, inp):
        qd, kd, u_c, w_c, att, dec = inp
        v_new = u_c - jnp.einsum('bhid,bhde->bhie', w_c, state)
        o = jnp.einsum('bhid,bhde->bhie', qd, state) + jnp.einsum('bhij,bhje->bhie', att, v_new)
        state = state * dec[..., None, None] + jnp.einsum('bhid,bhie->bhde', kd, v_new)
        return state, o

    xs = tuple(jnp.moveaxis(t, 2, 0) for t in (q_dec, k_dec, u, w, intra, chunk_decay))
    _, o = lax.scan(step, jnp.zeros((Bsz, H, DK, DV), f32), xs)
    o = jnp.transpose(o, (1, 0, 3, 2, 4)).reshape(Bsz, S, H, DV)
    o = o * lax.rsqrt(jnp.mean(jnp.square(o), -1, keepdims=True) + RMS_EPS) * norm_w
    o = o * jax.nn.silu(z.astype(f32).reshape(Bsz, S, H, DV))
    return o.reshape(Bsz, S, H * DV).astype(z.dtype)


def hybrid_mixer(h, rope, w_in, ssm_conv_w, ssm_conv_b, ssm_A_log, ssm_dt_bias, ssm_D, ssm_norm_w,
                 gdn_conv_w, gdn_A_log, gdn_dt_bias, gdn_norm_w, w_gate_up, b_gate, w_branch, w_out):
    B, S, D = h.shape
    cos, sin, cos_i, sin_i = rope
    proj = h @ w_in
    (aq, ak, av, iq, ik, iw, sz, sxbc, sdt, gqkv, gz, ga, gb, glat) = jnp.split(proj, _in_offsets(), axis=-1)
    y_a = sparse_indexed_attention(aq, ak, av, iq, ik, iw, cos, sin, cos_i, sin_i)
    y_b = mamba2_ssd(sz, sxbc, sdt, ssm_conv_w, ssm_conv_b, ssm_A_log, ssm_dt_bias, ssm_D, ssm_norm_w)
    y_c = gated_deltanet(gqkv, gz, ga, gb, gdn_conv_w, gdn_A_log, gdn_dt_bias, gdn_norm_w)
    gates = jax.nn.sigmoid((glat @ w_gate_up + b_gate).astype(jnp.float32)).astype(h.dtype)
    gates = gates.reshape(B, S, N_BRANCH, D)
    merged = gates[:, :, 0] * (y_a @ w_branch[0])
    merged = merged + gates[:, :, 1] * (y_b @ w_branch[1])
    merged = merged + gates[:, :, 2] * (y_c @ w_branch[2])
    return merged @ w_out


def swiglu_ffn(h, w_ffn_in, w_ffn_out):
    gate, up = jnp.split(h @ w_ffn_in, 2, axis=-1)
    return (jax.nn.silu(gate) * up) @ w_ffn_out


def _dt_bias(key, shape):
    dt = jnp.exp(jax.random.uniform(key, shape, minval=math.log(1e-3), maxval=math.log(1e-1)))
    return dt + jnp.log(-jnp.expm1(-dt))


def setup_inputs(seed: int = 0) -> dict:
    key = jax.random.key(seed)
    ks = jax.random.split(key, 24)
    f32 = jnp.float32
    L, D = DEPTH, D_MODEL
    nrm = lambda k, shape, scale: jax.random.normal(k, shape, f32) * scale
    positions = (jnp.arange(SEQ, dtype=jnp.int32)[None, :]
                 + jax.random.randint(ks[1], (BATCH, 1), 0, SEQ, dtype=jnp.int32))
    return {
        'x': jax.random.normal(ks[0], (BATCH, SEQ, D), f32),
        'positions': positions,
        'w_in': nrm(ks[2], (L, D, IN_WIDTH), D ** -0.5),
        'ssm_conv_w': nrm(ks[3], (L, SSM_CONV, SSM_CONV_DIM), SSM_CONV ** -0.5),
        'ssm_conv_b': nrm(ks[4], (L, SSM_CONV_DIM), 0.01),
        'ssm_A_log': jnp.log(jax.random.uniform(ks[5], (L, SSM_HEADS), f32, 1.0, 16.0)),
        'ssm_dt_bias': _dt_bias(ks[6], (L, SSM_HEADS)),
        'ssm_D': 1.0 + nrm(ks[7], (L, SSM_HEADS), 0.01),
        'ssm_norm_w': 1.0 + nrm(ks[8], (L, SSM_D_INNER), 0.01),
        'gdn_conv_w': nrm(ks[9], (L, GDN_CONV, GDN_CONV_DIM), GDN_CONV ** -0.5),
        'gdn_A_log': jnp.log(jax.random.uniform(ks[10], (L, GDN_HEADS), f32, 1.0, 16.0)),
        'gdn_dt_bias': _dt_bias(ks[11], (L, GDN_HEADS)),
        'gdn_norm_w': 1.0 + nrm(ks[12], (L, GDN_DV), 0.01),
        'w_gate_up': nrm(ks[13], (L, GATE_RANK, N_BRANCH * D), GATE_RANK ** -0.5),
        'b_gate': nrm(ks[14], (L, N_BRANCH * D), 0.01),
        'w_branch': nrm(ks[15], (L, N_BRANCH, BRANCH_WIDTH, D), BRANCH_WIDTH ** -0.5),
        'w_out': nrm(ks[16], (L, D, D), D ** -0.5 * DEEPNORM_BETA),
        'ln1_g': 1.0 + nrm(ks[17], (L, D), 0.01),
        'ln1_b': nrm(ks[18], (L, D), 0.01),
        'w_ffn_in': nrm(ks[19], (L, D, 2 * D_FF), D ** -0.5),
        'w_ffn_out': nrm(ks[20], (L, D_FF, D), D_FF ** -0.5 * DEEPNORM_BETA),
        'ln2_g': 1.0 + nrm(ks[21], (L, D), 0.01),
        'ln2_b': nrm(ks[22], (L, D), 0.01),
    }


def reference(x, positions, w_in, ssm_conv_w, ssm_conv_b, ssm_A_log, ssm_dt_bias, ssm_D, ssm_norm_w,
              gdn_conv_w, gdn_A_log, gdn_dt_bias, gdn_norm_w, w_gate_up, b_gate, w_branch, w_out,
              ln1_g, ln1_b, w_ffn_in, w_ffn_out, ln2_g, ln2_b):
    cos, sin = rope_tables(positions, HEAD_DIM)
    cos_i, sin_i = rope_tables(positions, IDX_DIM)
    rope = (cos, sin, cos_i, sin_i)
    for l in range(DEPTH):
        mix = hybrid_mixer(x, rope, w_in[l], ssm_conv_w[l], ssm_conv_b[l], ssm_A_log[l], ssm_dt_bias[l],
                           ssm_D[l], ssm_norm_w[l], gdn_conv_w[l], gdn_A_log[l], gdn_dt_bias[l],
                           gdn_norm_w[l], w_gate_up[l], b_gate[l], w_branch[l], w_out[l])
        x = layer_norm(DEEPNORM_ALPHA * x + mix, ln1_g[l], ln1_b[l])
        x = layer_norm(DEEPNORM_ALPHA * x + swiglu_ffn(x, w_ffn_in[l], w_ffn_out[l]), ln2_g[l], ln2_b[l])
    return x
```

```python
import functools

import jax
import jax.numpy as jnp
import numpy as np
from jax import lax
from jax.experimental import pallas as pl
from jax.experimental.pallas import tpu as pltpu

F32 = jnp.float32
BF16 = jnp.bfloat16

D_MODEL = 4096
DEPTH = 4
HEAD_DIM = 128
ROPE_THETA = 10000.0
LN_EPS = 1e-5
RMS_EPS = 1e-6
ATT_HEADS = 12
ATT_KV_HEADS = 4
IDX_HEADS = 16
IDX_DIM = 64
TOPK_MAX = 256
Q_BLOCK = 128
SSM_D_INNER = 1536
SSM_HEAD_DIM = 64
SSM_HEADS = SSM_D_INNER // SSM_HEAD_DIM
SSM_GROUPS = 4
SSM_STATE = 128
SSM_CONV = 4
SSM_CHUNK = 128
SSM_CONV_DIM = SSM_D_INNER + 2 * SSM_GROUPS * SSM_STATE
GDN_HEADS = 12
GDN_DK = 128
GDN_DV = 128
GDN_CONV = 4
GDN_CHUNK = 64
GDN_CONV_DIM = GDN_HEADS * (2 * GDN_DK + GDN_DV)
N_BRANCH = 3
BRANCH_WIDTH = 1536
GATE_RANK = 512
D_FF = -(-8 * D_MODEL // (3 * 256)) * 256
DEEPNORM_ALPHA = (2.0 * DEPTH) ** 0.25

IN_SPLITS = (
    ATT_HEADS * HEAD_DIM, ATT_KV_HEADS * HEAD_DIM, ATT_KV_HEADS * HEAD_DIM,
    IDX_HEADS * IDX_DIM, IDX_DIM, IDX_HEADS,
    SSM_D_INNER, SSM_CONV_DIM, SSM_HEADS,
    GDN_CONV_DIM, GDN_HEADS * GDN_DV, GDN_HEADS, GDN_HEADS,
    GATE_RANK,
)
IN_WIDTH = sum(IN_SPLITS)

V7X_VMEM_LIMIT_BYTES = 56 * 1024 * 1024


def _round_up(n, m):
    return -(-n // m) * m


def _mm_kernel(x_ref, w_ref, o_ref):
    o_ref[...] = jnp.dot(x_ref[...], w_ref[...], preferred_element_type=F32).astype(o_ref.dtype)


def matmul(x, w, *, tm, tn, out_dtype):
    M, K = x.shape
    _, N = w.shape
    return pl.pallas_call(
        _mm_kernel,
        grid=(M // tm, N // tn),
        in_specs=[pl.BlockSpec((tm, K), lambda i, j: (i, 0)),
                  pl.BlockSpec((K, tn), lambda i, j: (0, j))],
        out_specs=pl.BlockSpec((tm, tn), lambda i, j: (i, j)),
        out_shape=jax.ShapeDtypeStruct((M, N), out_dtype),
        compiler_params=pltpu.CompilerParams(
            dimension_semantics=("parallel", "arbitrary"),
            vmem_limit_bytes=V7X_VMEM_LIMIT_BYTES),
        name="matmul",
    )(x, w)


def _mm_res_kernel(x_ref, w_ref, r_ref, o_ref, acc_ref, *, alpha, nk):
    k = pl.program_id(2)

    @pl.when(k == 0)
    def _():
        acc_ref[...] = jnp.zeros_like(acc_ref)

    acc_ref[...] += jnp.dot(x_ref[...], w_ref[...], preferred_element_type=F32)

    @pl.when(k == nk - 1)
    def _():
        o_ref[...] = alpha * r_ref[...] + acc_ref[...]


def matmul_residual(x, w, res, *, alpha, tm, tn, tk):
    M, K = x.shape
    _, N = w.shape
    nk = K // tk
    return pl.pallas_call(
        functools.partial(_mm_res_kernel, alpha=alpha, nk=nk),
        grid=(M // tm, N // tn, nk),
        in_specs=[pl.BlockSpec((tm, tk), lambda i, j, k: (i, k)),
                  pl.BlockSpec((tk, tn), lambda i, j, k: (k, j)),
                  pl.BlockSpec((tm, tn), lambda i, j, k: (i, j))],
        out_specs=pl.BlockSpec((tm, tn), lambda i, j, k: (i, j)),
        out_shape=jax.ShapeDtypeStruct((M, N), F32),
        scratch_shapes=[pltpu.VMEM((tm, tn), F32)],
        compiler_params=pltpu.CompilerParams(
            dimension_semantics=("parallel", "arbitrary", "arbitrary"),
            vmem_limit_bytes=V7X_VMEM_LIMIT_BYTES),
        name="matmul_residual",
    )(x, w, res)


def _swiglu_kernel(x_ref, wg_ref, wu_ref, o_ref):
    x = x_ref[...]
    g = jnp.dot(x, wg_ref[...], preferred_element_type=F32)
    u = jnp.dot(x, wu_ref[...], preferred_element_type=F32)
    o_ref[...] = (g * jax.nn.sigmoid(g) * u).astype(o_ref.dtype)


def swiglu_in(x, w_in, *, tm, tn):
    M, K = x.shape
    F = w_in.shape[1] // 2
    nj = F // tn
    return pl.pallas_call(
        _swiglu_kernel,
        grid=(M // tm, nj),
        in_specs=[pl.BlockSpec((tm, K), lambda i, j: (i, 0)),
                  pl.BlockSpec((K, tn), lambda i, j: (0, j)),
                  pl.BlockSpec((K, tn), lambda i, j: (0, j + nj))],
        out_specs=pl.BlockSpec((tm, tn), lambda i, j: (i, j)),
        out_shape=jax.ShapeDtypeStruct((M, F), BF16),
        compiler_params=pltpu.CompilerParams(
            dimension_semantics=("parallel", "arbitrary"),
            vmem_limit_bytes=V7X_VMEM_LIMIT_BYTES),
        name="swiglu_in",
    )(x, w_in, w_in)


def _merge_kernel(ya_ref, yb_ref, yc_ref, gl_ref, wb_ref, wg0_ref, wg1_ref, wg2_ref, bg_ref, o_ref):
    gl = gl_ref[...]
    acc = None
    for i, (y_ref, wg_ref) in enumerate(((ya_ref, wg0_ref), (yb_ref, wg1_ref), (yc_ref, wg2_ref))):
        gate = jax.nn.sigmoid(jnp.dot(gl, wg_ref[...], preferred_element_type=F32) + bg_ref[i])
        term = gate * jnp.dot(y_ref[...], wb_ref[i], preferred_element_type=F32)
        acc = term if acc is None else acc + term
    o_ref[...] = acc.astype(o_ref.dtype)


def branch_merge(ya, yb, yc, glat, w_branch, w_gate_up, b_gate, *, tm, tn):
    M = ya.shape[0]
    D = w_branch.shape[-1]
    nj = D // tn
    ysp = pl.BlockSpec((tm, BRANCH_WIDTH), lambda i, j: (i, 0))
    wg_specs = [pl.BlockSpec((GATE_RANK, tn), functools.partial(lambda i, j, b: (0, j + b * nj), b=b))
                for b in range(N_BRANCH)]
    return pl.pallas_call(
        _merge_kernel,
        grid=(M // tm, nj),
        in_specs=[ysp, ysp, ysp,
                  pl.BlockSpec((tm, GATE_RANK), lambda i, j: (i, 0)),
                  pl.BlockSpec((N_BRANCH, BRANCH_WIDTH, tn), lambda i, j: (0, 0, j)),
                  *wg_specs,
                  pl.BlockSpec((N_BRANCH, 1, tn), lambda i, j: (0, 0, j))],
        out_specs=pl.BlockSpec((tm, tn), lambda i, j: (i, j)),
        out_shape=jax.ShapeDtypeStruct((M, D), BF16),
        compiler_params=pltpu.CompilerParams(
            dimension_semantics=("parallel", "arbitrary"),
            vmem_limit_bytes=V7X_VMEM_LIMIT_BYTES),
        name="branch_merge",
    )(ya, yb, yc, glat, w_branch, w_gate_up, w_gate_up, w_gate_up, b_gate.reshape(N_BRANCH, 1, D))


def _ln_kernel(z_ref, g_ref, b_ref, o_ref, obf_ref):
    z = z_ref[...]
    mu = jnp.mean(z, -1, keepdims=True)
    zc = z - mu
    var = jnp.mean(zc * zc, -1, keepdims=True)
    y = zc * lax.rsqrt(var + LN_EPS) * g_ref[...] + b_ref[...]
    o_ref[...] = y
    obf_ref[...] = y.astype(BF16)


def layer_norm(z, g, b, *, tr):
    M, D = z.shape
    return pl.pallas_call(
        _ln_kernel,
        grid=(M // tr,),
        in_specs=[pl.BlockSpec((tr, D), lambda i: (i, 0)),
                  pl.BlockSpec((1, D), lambda i: (0, 0)),
                  pl.BlockSpec((1, D), lambda i: (0, 0))],
        out_specs=[pl.BlockSpec((tr, D), lambda i: (i, 0)),
                   pl.BlockSpec((tr, D), lambda i: (i, 0))],
        out_shape=[jax.ShapeDtypeStruct((M, D), F32), jax.ShapeDtypeStruct((M, D), BF16)],
        compiler_params=pltpu.CompilerParams(
            dimension_semantics=("parallel",),
            vmem_limit_bytes=V7X_VMEM_LIMIT_BYTES),
        name="layer_norm",
    )(z, g.reshape(1, D), b.reshape(1, D))


def rope_tables(positions, dim):
    inv = ROPE_THETA ** (-jnp.arange(0, dim, 2, dtype=F32) / dim)
    ang = positions.astype(F32)[..., None] * inv
    return jnp.cos(ang), jnp.sin(ang)


def apply_rope(x, cos, sin):
    x1, x2 = jnp.split(x.astype(F32), 2, axis=-1)
    c, s = cos[:, :, None], sin[:, :, None]
    return jnp.concatenate([x1 * c - x2 * s, x2 * c + x1 * s], -1).astype(x.dtype)


def causal_dwconv(x, w, b=None):
    K, C = w.shape
    y = lax.conv_general_dilated(x, w[:, None, :].astype(x.dtype), window_strides=(1,),
                                 padding=[(K - 1, 0)], dimension_numbers=('NWC', 'WIO', 'NWC'),
                                 feature_group_count=C)
    return y if b is None else y + b.astype(x.dtype)


def l2norm(x):
    return x * lax.rsqrt(jnp.sum(jnp.square(x), -1, keepdims=True) + RMS_EPS)


def sparse_indexed_attention(q, k, v, iq, ik, iw, cos, sin, cos_i, sin_i):
    B, S = q.shape[:2]
    grp = ATT_HEADS // ATT_KV_HEADS
    q = apply_rope(q.reshape(B, S, ATT_HEADS, HEAD_DIM), cos, sin)
    k = apply_rope(k.reshape(B, S, ATT_KV_HEADS, HEAD_DIM), cos, sin)
    v = v.reshape(B, S, ATT_KV_HEADS, HEAD_DIM)
    iq = apply_rope(iq.reshape(B, S, IDX_HEADS, IDX_DIM), cos_i, sin_i)
    ik = apply_rope(ik.reshape(B, S, 1, IDX_DIM), cos_i, sin_i)[:, :, 0].astype(F32)
    iw = iw.astype(F32) * IDX_HEADS ** -0.5
    topk = min(TOPK_MAX, S // 4)
    nb = S // Q_BLOCK
    key_pos = jnp.arange(S)

    def to_blocks(a):
        return jnp.moveaxis(a.reshape(B, nb, Q_BLOCK, *a.shape[2:]), 1, 0)

    def block(args):
        qb, iqb, iwb, start = args
        qpos = start + jnp.arange(Q_BLOCK)
        logits = jnp.einsum('bthd,bsd->bths', iqb.astype(F32), ik) * IDX_DIM ** -0.5
        score = jnp.einsum('bths,bth->bts', jax.nn.relu(logits), iwb)
        causal = key_pos[None, :] <= qpos[:, None]
        score = jnp.where(causal[None], score, -jnp.inf)
        _, idx = lax.top_k(score, topk)
        valid = idx <= qpos[None, :, None]
        ksel = jax.vmap(lambda kb, ib: kb[ib])(k, idx)
        vsel = jax.vmap(lambda vb, ib: vb[ib])(v, idx)
        qg = qb.reshape(B, Q_BLOCK, ATT_KV_HEADS, grp, HEAD_DIM)
        s = jnp.einsum('btgrd,btkgd->btgrk', qg, ksel, preferred_element_type=F32) * HEAD_DIM ** -0.5
        s = jnp.where(valid[:, :, None, None, :], s, -jnp.inf)
        p = jax.nn.softmax(s, axis=-1)
        o = jnp.einsum('btgrk,btkgd->btgrd', p.astype(v.dtype), vsel)
        return o.reshape(B, Q_BLOCK, ATT_HEADS * HEAD_DIM)

    starts = jnp.arange(nb, dtype=jnp.int32) * Q_BLOCK
    out = lax.map(block, (to_blocks(q), to_blocks(iq), to_blocks(iw), starts))
    return jnp.moveaxis(out, 0, 1).reshape(B, S, ATT_HEADS * HEAD_DIM).astype(v.dtype)


def mamba2_ssd(z, xbc, dt_raw, conv_w, conv_b, A_log, dt_bias, D_skip, norm_w):
    Bsz, S, _ = z.shape
    G, R, P, N, Q = SSM_GROUPS, SSM_HEADS // SSM_GROUPS, SSM_HEAD_DIM, SSM_STATE, SSM_CHUNK
    nc = S // Q
    xbc = jax.nn.silu(causal_dwconv(xbc, conv_w, conv_b)).astype(F32)
    xs, Bm, Cm = jnp.split(xbc, [SSM_D_INNER, SSM_D_INNER + G * N], axis=-1)
    dt = jax.nn.softplus(dt_raw.astype(F32) + dt_bias.astype(F32))
    A = -jnp.exp(A_log.astype(F32))
    X = xs.reshape(Bsz, nc, Q, G, R, P)
    dtc = dt.reshape(Bsz, nc, Q, G, R)
    Xdt = X * dtc[..., None]
    Bc = Bm.reshape(Bsz, nc, Q, G, N)
    Cc = Cm.reshape(Bsz, nc, Q, G, N)
    a_cs = jnp.cumsum(dtc * A.reshape(G, R), axis=2)
    causal = jnp.tril(jnp.ones((Q, Q), dtype=bool))[:, :, None, None]
    seg = a_cs[:, :, :, None] - a_cs[:, :, None, :]
    Lmat = jnp.exp(jnp.where(causal, seg, -jnp.inf))
    CB = jnp.einsum('bclgn,bcsgn->bclsg', Cc, Bc)
    y_diag = jnp.einsum('bclsgr,bcsgrp->bclgrp', CB[..., None] * Lmat, Xdt)
    states = jnp.einsum('bclgn,bclgr,bclgrp->bcgrpn', Bc, jnp.exp(a_cs[:, :, -1:] - a_cs), Xdt)
    chunk_decay = jnp.exp(a_cs[:, :, -1])

    def carry_state(h, inp):
        st, dec = inp
        return h * dec[..., None, None] + st, h

    h0 = jnp.zeros((Bsz, G, R, P, N), F32)
    _, h_prev = lax.scan(carry_state, h0, (jnp.moveaxis(states, 1, 0), jnp.moveaxis(chunk_decay, 1, 0)))
    h_prev = jnp.moveaxis(h_prev, 0, 1)
    y_off = jnp.einsum('bclgn,bcgrpn,bclgr->bclgrp', Cc, h_prev, jnp.exp(a_cs))
    y = y_diag + y_off + X * D_skip.astype(F32).reshape(G, R, 1)
    y = y.reshape(Bsz, S, SSM_D_INNER) * jax.nn.silu(z.astype(F32))
    yg = y.reshape(Bsz, S, G, SSM_D_INNER // G)
    yg = yg * lax.rsqrt(jnp.mean(jnp.square(yg), -1, keepdims=True) + RMS_EPS)
    return (yg.reshape(Bsz, S, SSM_D_INNER) * norm_w).astype(z.dtype)


def gated_deltanet(qkv, z, a_raw, b_raw, conv_w, A_log, dt_bias, norm_w):
    Bsz, S, _ = qkv.shape
    H, DK, DV, C = GDN_HEADS, GDN_DK, GDN_DV, GDN_CHUNK
    nc = S // C
    qkv = jax.nn.silu(causal_dwconv(qkv, conv_w)).astype(F32)
    q, k, v = jnp.split(qkv, [H * DK, 2 * H * DK], axis=-1)
    q = l2norm(q.reshape(Bsz, S, H, DK)) * DK ** -0.5
    k = l2norm(k.reshape(Bsz, S, H, DK))
    v = v.reshape(Bsz, S, H, DV)
    beta = jax.nn.sigmoid(b_raw.astype(F32))
    g = -jnp.exp(A_log.astype(F32)) * jax.nn.softplus(a_raw.astype(F32) + dt_bias.astype(F32))

    def chunks(t):
        return jnp.moveaxis(t.reshape(Bsz, nc, C, H, *t.shape[3:]), 3, 1)

    qc, kc, vc, bc = chunks(q), chunks(k), chunks(v), chunks(beta)
    g_cs = jnp.cumsum(chunks(g), axis=-1)
    tril = jnp.tril(jnp.ones((C, C), dtype=bool))
    strict = jnp.tril(jnp.ones((C, C), dtype=bool), -1)
    decay = jnp.exp(jnp.where(tril, g_cs[..., :, None] - g_cs[..., None, :], -jnp.inf))
    kb = kc * bc[..., None]
    lower = jnp.where(strict, jnp.einsum('bhcid,bhcjd->bhcij', kb, kc) * decay, 0.0)
    eye = jnp.eye(C, dtype=F32)
    T = lax.linalg.triangular_solve(lower + eye, jnp.broadcast_to(eye, lower.shape),
                                    left_side=True, lower=True, unit_diagonal=True)
    u = jnp.einsum('bhcij,bhcje->bhcie', T, vc * bc[..., None])
    w = jnp.einsum('bhcij,bhcjd->bhcid', T, kb * jnp.exp(g_cs)[..., None])
    intra = jnp.einsum('bhcid,bhcjd->bhcij', qc, kc) * decay
    q_dec = qc * jnp.exp(g_cs)[..., None]
    k_dec = kc * jnp.exp(g_cs[..., -1:] - g_cs)[..., None]
    chunk_decay = jnp.exp(g_cs[..., -1])

    def step(state, inp):
        qd, kd, u_c, w_c, att, dec = inp
        v_new = u_c - jnp.einsum('bhid,bhde->bhie', w_c, state)
        o = jnp.einsum('bhid,bhde->bhie', qd, state) + jnp.einsum('bhij,bhje->bhie', att, v_new)
        state = state * dec[..., None, None] + jnp.einsum('bhid,bhie->bhde', kd, v_new)
        return state, o

    xs = tuple(jnp.moveaxis(t, 2, 0) for t in (q_dec, k_dec, u, w, intra, chunk_decay))
    _, o = lax.scan(step, jnp.zeros((Bsz, H, DK, DV), F32), xs)
    o = jnp.transpose(o, (1, 0, 3, 2, 4)).reshape(Bsz, S, H, DV)
    o = o * lax.rsqrt(jnp.mean(jnp.square(o), -1, keepdims=True) + RMS_EPS) * norm_w
    o = o * jax.nn.silu(z.astype(F32).reshape(Bsz, S, H, DV))
    return o.reshape(Bsz, S, H * DV).astype(z.dtype)


IN_WIDTH_PADDED = _round_up(IN_WIDTH, 512)


def kernel(x, positions, w_in, ssm_conv_w, ssm_conv_b, ssm_A_log, ssm_dt_bias, ssm_D, ssm_norm_w,
           gdn_conv_w, gdn_A_log, gdn_dt_bias, gdn_norm_w, w_gate_up, b_gate, w_branch, w_out,
           ln1_g, ln1_b, w_ffn_in, w_ffn_out, ln2_g, ln2_b):
    B, S, D = x.shape
    M = B * S
    cos, sin = rope_tables(positions, HEAD_DIM)
    cos_i, sin_i = rope_tables(positions, IDX_DIM)
    offs = [int(o) for o in np.cumsum(IN_SPLITS)[:-1]]

    xf = x.reshape(M, D)
    xb = xf.astype(BF16)
    for l in range(DEPTH):
        w_in_l = jnp.pad(w_in[l].astype(BF16), ((0, 0), (0, IN_WIDTH_PADDED - IN_WIDTH)))
        proj = matmul(xb, w_in_l, tm=1024, tn=512, out_dtype=F32)
        proj = proj[:, :IN_WIDTH].reshape(B, S, IN_WIDTH)
        (aq, ak, av, iq, ik, iw, sz, sxbc, sdt, gqkv, gz, ga, gb, glat) = jnp.split(proj, offs, axis=-1)
        y_a = sparse_indexed_attention(aq, ak, av, iq, ik, iw, cos, sin, cos_i, sin_i)
        y_b = mamba2_ssd(sz, sxbc, sdt, ssm_conv_w[l], ssm_conv_b[l], ssm_A_log[l], ssm_dt_bias[l],
                         ssm_D[l], ssm_norm_w[l])
        y_c = gated_deltanet(gqkv, gz, ga, gb, gdn_conv_w[l], gdn_A_log[l], gdn_dt_bias[l], gdn_norm_w[l])
        merged = branch_merge(y_a.reshape(M, -1).astype(BF16), y_b.reshape(M, -1).astype(BF16),
                              y_c.reshape(M, -1).astype(BF16), glat.reshape(M, -1).astype(BF16),
                              w_branch[l].astype(BF16), w_gate_up[l].astype(BF16), b_gate[l],
                              tm=1024, tn=512)
        z = matmul_residual(merged, w_out[l].astype(BF16), xf, alpha=DEEPNORM_ALPHA, tm=1024, tn=512, tk=D)
        xf, xb = layer_norm(z, ln1_g[l], ln1_b[l], tr=256)
        h = swiglu_in(xb, w_ffn_in[l].astype(BF16), tm=1024, tn=256)
        z = matmul_residual(h, w_ffn_out[l].astype(BF16), xf, alpha=DEEPNORM_ALPHA, tm=1024, tn=512,
                            tk=D_FF // 2)
        xf, xb = layer_norm(z, ln2_g[l], ln2_b[l], tr=256)
    return xf.reshape(B, S, D)
```

```python
import functools

import jax
import jax.numpy as jnp
import numpy as np
from jax import lax
from jax.experimental import pallas as pl
from jax.experimental.pallas import tpu as pltpu

F32 = jnp.float32
BF16 = jnp.bfloat16

D_MODEL = 4096
DEPTH = 4
HEAD_DIM = 128
ROPE_THETA = 10000.0
LN_EPS = 1e-5
RMS_EPS = 1e-6
ATT_HEADS = 12
ATT_KV_HEADS = 4
IDX_HEADS = 16
IDX_DIM = 64
TOPK_MAX = 256
Q_BLOCK = 128
SSM_D_INNER = 1536
SSM_HEAD_DIM = 64
SSM_HEADS = SSM_D_INNER // SSM_HEAD_DIM
SSM_GROUPS = 4
SSM_STATE = 128
SSM_CONV = 4
SSM_CHUNK = 128
SSM_CONV_DIM = SSM_D_INNER + 2 * SSM_GROUPS * SSM_STATE
GDN_HEADS = 12
GDN_DK = 128
GDN_DV = 128
GDN_CONV = 4
GDN_CHUNK = 64
GDN_CONV_DIM = GDN_HEADS * (2 * GDN_DK + GDN_DV)
N_BRANCH = 3
BRANCH_WIDTH = 1536
GATE_RANK = 512
D_FF = -(-8 * D_MODEL // (3 * 256)) * 256
DEEPNORM_ALPHA = (2.0 * DEPTH) ** 0.25

IN_SPLITS = (
    ATT_HEADS * HEAD_DIM, ATT_KV_HEADS * HEAD_DIM, ATT_KV_HEADS * HEAD_DIM,
    IDX_HEADS * IDX_DIM, IDX_DIM, IDX_HEADS,
    SSM_D_INNER, SSM_CONV_DIM, SSM_HEADS,
    GDN_CONV_DIM, GDN_HEADS * GDN_DV, GDN_HEADS, GDN_HEADS,
    GATE_RANK,
)
IN_WIDTH = sum(IN_SPLITS)

V7X_VMEM_LIMIT_BYTES = 56 * 1024 * 1024


def _round_up(n, m):
    return -(-n // m) * m


def _mm_kernel(x_ref, w_ref, o_ref):
    o_ref[...] = jnp.dot(x_ref[...], w_ref[...], preferred_element_type=F32).astype(o_ref.dtype)


def matmul(x, w, *, tm, tn, out_dtype):
    M, K = x.shape
    _, N = w.shape
    return pl.pallas_call(
        _mm_kernel,
        grid=(M // tm, N // tn),
        in_specs=[pl.BlockSpec((tm, K), lambda i, j: (i, 0)),
                  pl.BlockSpec((K, tn), lambda i, j: (0, j))],
        out_specs=pl.BlockSpec((tm, tn), lambda i, j: (i, j)),
        out_shape=jax.ShapeDtypeStruct((M, N), out_dtype),
        compiler_params=pltpu.CompilerParams(
            dimension_semantics=("parallel", "arbitrary"),
            vmem_limit_bytes=V7X_VMEM_LIMIT_BYTES),
        name="matmul",
    )(x, w)


def _mm_res_kernel(x_ref, w_ref, r_ref, o_ref, acc_ref, *, alpha, nk):
    k = pl.program_id(2)

    @pl.when(k == 0)
    def _():
        acc_ref[...] = jnp.zeros_like(acc_ref)

    acc_ref[...] += jnp.dot(x_ref[...], w_ref[...], preferred_element_type=F32)

    @pl.when(k == nk - 1)
    def _():
        o_ref[...] = alpha * r_ref[...] + acc_ref[...]


def matmul_residual(x, w, res, *, alpha, tm, tn, tk):
    M, K = x.shape
    _, N = w.shape
    nk = K // tk
    return pl.pallas_call(
        functools.partial(_mm_res_kernel, alpha=alpha, nk=nk),
        grid=(M // tm, N // tn, nk),
        in_specs=[pl.BlockSpec((tm, tk), lambda i, j, k: (i, k)),
                  pl.BlockSpec((tk, tn), lambda i, j, k: (k, j)),
                  pl.BlockSpec((tm, tn), lambda i, j, k: (i, j))],
        out_specs=pl.BlockSpec((tm, tn), lambda i, j, k: (i, j)),
        out_shape=jax.ShapeDtypeStruct((M, N), F32),
        scratch_shapes=[pltpu.VMEM((tm, tn), F32)],
        compiler_params=pltpu.CompilerParams(
            dimension_semantics=("parallel", "arbitrary", "arbitrary"),
            vmem_limit_bytes=V7X_VMEM_LIMIT_BYTES),
        name="matmul_residual",
    )(x, w, res)


def _swiglu_kernel(x_ref, wg_ref, wu_ref, o_ref):
    x = x_ref[...]
    g = jnp.dot(x, wg_ref[...], preferred_element_type=F32)
    u = jnp.dot(x, wu_ref[...], preferred_element_type=F32)
    o_ref[...] = (g * jax.nn.sigmoid(g) * u).astype(o_ref.dtype)


def swiglu_in(x, w_in, *, tm, tn):
    M, K = x.shape
    F = w_in.shape[1] // 2
    nj = F // tn
    return pl.pallas_call(
        _swiglu_kernel,
        grid=(M // tm, nj),
        in_specs=[pl.BlockSpec((tm, K), lambda i, j: (i, 0)),
                  pl.BlockSpec((K, tn), lambda i, j: (0, j)),
                  pl.BlockSpec((K, tn), lambda i, j: (0, j + nj))],
        out_specs=pl.BlockSpec((tm, tn), lambda i, j: (i, j)),
        out_shape=jax.ShapeDtypeStruct((M, F), BF16),
        compiler_params=pltpu.CompilerParams(
            dimension_semantics=("parallel", "arbitrary"),
            vmem_limit_bytes=V7X_VMEM_LIMIT_BYTES),
        name="swiglu_in",
    )(x, w_in, w_in)


def _merge_kernel(ya_ref, yb_ref, yc_ref, gl_ref, wb_ref, wg0_ref, wg1_ref, wg2_ref, bg_ref, o_ref):
    gl = gl_ref[...].astype(BF16)
    acc = None
    for i, (y_ref, wg_ref) in enumerate(((ya_ref, wg0_ref), (yb_ref, wg1_ref), (yc_ref, wg2_ref))):
        gate = jax.nn.sigmoid(jnp.dot(gl, wg_ref[...], preferred_element_type=F32) + bg_ref[i])
        term = gate * jnp.dot(y_ref[...], wb_ref[i], preferred_element_type=F32)
        acc = term if acc is None else acc + term
    o_ref[...] = acc.astype(o_ref.dtype)


def branch_merge(ya, yb, yc, proj, w_branch, w_gate_up, b_gate, *, tm, tn):
    M = ya.shape[0]
    D = w_branch.shape[-1]
    nj = D // tn
    glat_block = SEG["glat"][0] // GATE_RANK
    ysp = pl.BlockSpec((tm, BRANCH_WIDTH), lambda i, j: (i, 0))
    wg_specs = [pl.BlockSpec((GATE_RANK, tn), functools.partial(lambda i, j, b: (0, j + b * nj), b=b))
                for b in range(N_BRANCH)]
    return pl.pallas_call(
        _merge_kernel,
        grid=(M // tm, nj),
        in_specs=[ysp, ysp, ysp,
                  pl.BlockSpec((tm, GATE_RANK), lambda i, j: (i, glat_block)),
                  pl.BlockSpec((N_BRANCH, BRANCH_WIDTH, tn), lambda i, j: (0, 0, j)),
                  *wg_specs,
                  pl.BlockSpec((N_BRANCH, 1, tn), lambda i, j: (0, 0, j))],
        out_specs=pl.BlockSpec((tm, tn), lambda i, j: (i, j)),
        out_shape=jax.ShapeDtypeStruct((M, D), BF16),
        compiler_params=pltpu.CompilerParams(
            dimension_semantics=("parallel", "arbitrary"),
            vmem_limit_bytes=V7X_VMEM_LIMIT_BYTES),
        name="branch_merge",
    )(ya, yb, yc, proj, w_branch, w_gate_up, w_gate_up, w_gate_up, b_gate.reshape(N_BRANCH, 1, D))


def _ln_kernel(z_ref, g_ref, b_ref, o_ref, obf_ref):
    z = z_ref[...]
    mu = jnp.mean(z, -1, keepdims=True)
    zc = z - mu
    var = jnp.mean(zc * zc, -1, keepdims=True)
    y = zc * lax.rsqrt(var + LN_EPS) * g_ref[...] + b_ref[...]
    o_ref[...] = y
    obf_ref[...] = y.astype(BF16)


def layer_norm(z, g, b, *, tr):
    M, D = z.shape
    return pl.pallas_call(
        _ln_kernel,
        grid=(M // tr,),
        in_specs=[pl.BlockSpec((tr, D), lambda i: (i, 0)),
                  pl.BlockSpec((1, D), lambda i: (0, 0)),
                  pl.BlockSpec((1, D), lambda i: (0, 0))],
        out_specs=[pl.BlockSpec((tr, D), lambda i: (i, 0)),
                   pl.BlockSpec((tr, D), lambda i: (i, 0))],
        out_shape=[jax.ShapeDtypeStruct((M, D), F32), jax.ShapeDtypeStruct((M, D), BF16)],
        compiler_params=pltpu.CompilerParams(
            dimension_semantics=("parallel",),
            vmem_limit_bytes=V7X_VMEM_LIMIT_BYTES),
        name="layer_norm",
    )(z, g.reshape(1, D), b.reshape(1, D))


def rope_tables(positions, dim):
    inv = ROPE_THETA ** (-jnp.arange(0, dim, 2, dtype=F32) / dim)
    ang = positions.astype(F32)[..., None] * inv
    return jnp.cos(ang), jnp.sin(ang)


_LAYOUT_ORDER = (
    ("aq", 1536), ("sz", 1536), ("gz", 1536), ("gq", 1536), ("gk", 1536), ("gv", 1536), ("sx", 1536),
    ("ak", 512), ("av", 512), ("sB", 512), ("sC", 512), ("glat", 512),
    ("iq", 1024), ("ik", 128), ("iw", 128), ("sdt", 128), ("gab", 128),
)


def _build_layout():
    segs, off = {}, 0
    for name, width in _LAYOUT_ORDER:
        assert off % width == 0, name
        segs[name] = (off, width)
        off += width
    return segs, off


SEG, IN_WIDTH_PADDED = _build_layout()


def _col_block(name):
    off, width = SEG[name]
    return off // width


def permute_w_in(w, dtype=BF16):
    D = w.shape[0]
    o = [int(v) for v in np.cumsum((0,) + IN_SPLITS)]
    aq, ak, av, iq, ik, iw, sz, sxbc, sdt, gqkv, gz, ga, gb, glat = [w[:, o[i]:o[i + 1]] for i in range(14)]
    half = IDX_DIM // 2
    iq_p = iq.reshape(D, IDX_HEADS // 2, 2, 2, half).transpose(0, 1, 3, 2, 4).reshape(D, IDX_HEADS * IDX_DIM)
    ik_p = jnp.repeat(ik.reshape(D, 2, 1, half), 2, axis=2).reshape(D, 2 * IDX_DIM)

    def pad128(a):
        return jnp.pad(a, ((0, 0), (0, 128 - a.shape[1])))

    G, N = SSM_GROUPS, SSM_STATE
    pieces = {
        "aq": aq, "sz": sz, "gz": gz,
        "gq": gqkv[:, :GDN_HEADS * GDN_DK], "gk": gqkv[:, GDN_HEADS * GDN_DK:2 * GDN_HEADS * GDN_DK],
        "gv": gqkv[:, 2 * GDN_HEADS * GDN_DK:],
        "sx": sxbc[:, :SSM_D_INNER], "ak": ak, "av": av,
        "sB": sxbc[:, SSM_D_INNER:SSM_D_INNER + G * N], "sC": sxbc[:, SSM_D_INNER + G * N:],
        "glat": glat, "iq": iq_p, "ik": ik_p, "iw": pad128(iw), "sdt": pad128(sdt),
        "gab": pad128(jnp.concatenate([ga, gb], axis=1)),
    }
    return jnp.concatenate([pieces[name].astype(dtype) for name, _ in _LAYOUT_ORDER], axis=1)


def rope_lane_tables(positions):
    M = positions.size
    cos, sin = rope_tables(positions.reshape(M), HEAD_DIM)
    cos_i, sin_i = rope_tables(positions.reshape(M), IDX_DIM)
    return (jnp.concatenate([cos, cos], -1), jnp.concatenate([-sin, sin], -1),
            jnp.concatenate([cos_i] * 4, -1), jnp.concatenate([-sin_i, -sin_i, sin_i, sin_i], -1))


def _attn_prep_kernel(aq_ref, ak_ref, av_ref, iq_ref, ik_ref, iw_ref, c_ref, s_ref, ci_ref, si_ref,
                      q_out, k_out, v_out, iq_out, ik_out, iwt_out):
    c, s, ci, si = c_ref[...], s_ref[...], ci_ref[...], si_ref[...]

    def rope(x, cc, ss):
        return x * cc + pltpu.roll(x, 64, 1) * ss

    for h in range(ATT_HEADS):
        sl = slice(h * HEAD_DIM, (h + 1) * HEAD_DIM)
        q_out[:, sl] = (rope(aq_ref[:, sl], c, s) * HEAD_DIM ** -0.5).astype(q_out.dtype)
    for h in range(ATT_KV_HEADS):
        sl = slice(h * HEAD_DIM, (h + 1) * HEAD_DIM)
        k_out[:, sl] = rope(ak_ref[:, sl], c, s).astype(k_out.dtype)
    v_out[...] = av_ref[...].astype(v_out.dtype)
    for p in range(IDX_HEADS // 2):
        sl = slice(p * 128, (p + 1) * 128)
        iq_out[:, sl] = (rope(iq_ref[:, sl], ci, si) * IDX_DIM ** -0.5).astype(iq_out.dtype)
    ik_out[...] = rope(ik_ref[...], ci, si).astype(ik_out.dtype)
    iwt_out[...] = (iw_ref[...] * IDX_HEADS ** -0.5).T[:IDX_HEADS, :]


def attn_prep(proj, tables, *, tr, cdt):
    M = proj.shape[0]

    def seg(name):
        return pl.BlockSpec((tr, SEG[name][1]), functools.partial(lambda i, b: (i, b), b=_col_block(name)))

    tab = pl.BlockSpec((tr, 128), lambda i: (i, 0))
    row = lambda w: pl.BlockSpec((tr, w), lambda i: (i, 0))
    return pl.pallas_call(
        _attn_prep_kernel,
        grid=(M // tr,),
        in_specs=[seg("aq"), seg("ak"), seg("av"), seg("iq"), seg("ik"), seg("iw"), tab, tab, tab, tab],
        out_specs=[row(1536), row(512), row(512), row(1024), row(128),
                   pl.BlockSpec((IDX_HEADS, tr), lambda i: (0, i))],
        out_shape=[jax.ShapeDtypeStruct((M, 1536), cdt), jax.ShapeDtypeStruct((M, 512), cdt),
                   jax.ShapeDtypeStruct((M, 512), cdt), jax.ShapeDtypeStruct((M, 1024), cdt),
                   jax.ShapeDtypeStruct((M, 128), cdt), jax.ShapeDtypeStruct((IDX_HEADS, M), F32)],
        compiler_params=pltpu.CompilerParams(
            dimension_semantics=("parallel",), vmem_limit_bytes=V7X_VMEM_LIMIT_BYTES),
        name="attn_prep",
    )(proj, proj, proj, proj, proj, proj, *tables)


MASK_BIAS = -1e30
INT32_MIN = -2 ** 31
KEY_NEG_INF = int(np.array(-np.inf, np.float32).view(np.int32)) ^ 0x7FFFFFFF
KEY_SUB = 128


def _attn_kernel(q_ref, k_ref, v_ref, iq_ref, ik_ref, iwt_ref, o_ref,
                 iq2_ref, key_ref, bias_ref, j_ref, m_ref, l_ref, acc_ref, *, tq, topk, seq):
    qi = pl.program_id(1)
    nch = qi + 1
    n_pair = IDX_HEADS // 2
    grp = ATT_HEADS // ATT_KV_HEADS
    nt_dims = (((1,), (1,)), ((), ()))

    lane = lax.broadcasted_iota(jnp.int32, (tq, 128), 1)
    is_a = (lane & (IDX_DIM // 2)) == 0
    for p in range(n_pair):
        x = iq_ref[:, p * 128:(p + 1) * 128]
        zero = jnp.zeros_like(x)
        iq2_ref[p, :tq, :] = jnp.where(is_a, x, zero)
        iq2_ref[p, tq:, :] = jnp.where(is_a, zero, x)

    iwt = iwt_ref[...]
    q_pos = qi * tq + lax.broadcasted_iota(jnp.int32, (KEY_SUB, tq), 1)
    sub_iota = lax.broadcasted_iota(jnp.int32, (KEY_SUB, tq), 0)

    def score_chunk(c, carry):
        for sub in range(tq // KEY_SUB):
            r0 = pl.multiple_of(c * tq + sub * KEY_SUB, KEY_SUB)
            ikc = ik_ref[pl.ds(r0, KEY_SUB), :]
            acc = jnp.zeros((KEY_SUB, tq), F32)
            for p in range(n_pair):
                lt = lax.dot_general(ikc, iq2_ref[p], nt_dims, preferred_element_type=F32)
                acc = acc + iwt[2 * p:2 * p + 1, :] * jnp.maximum(lt[:, :tq], 0.0)
                acc = acc + iwt[2 * p + 1:2 * p + 2, :] * jnp.maximum(lt[:, tq:], 0.0)
            acc = jnp.where(r0 + sub_iota <= q_pos, acc, -jnp.inf)
            bits = lax.bitcast_convert_type(acc, jnp.int32)
            key_ref[c, sub * KEY_SUB:(sub + 1) * KEY_SUB, :] = bits ^ ((bits >> 31) & 0x7FFFFFFF)
        return carry

    lax.fori_loop(0, nch, score_chunk, 0)

    one = jnp.ones((tq, tq), jnp.int32)
    zero_i = jnp.zeros((tq, tq), jnp.int32)

    def count(indicator):
        def body(c, cnt8):
            ind = indicator(c, key_ref[c])
            return cnt8 + ind.reshape(tq // 8, 8, tq).sum(axis=0)
        cnt8 = lax.fori_loop(0, nch, body, jnp.zeros((8, tq), jnp.int32))
        return cnt8.sum(axis=0, keepdims=True)

    def bit_body(i, carry):
        lo, cnt_lo = carry
        cand = lo + jnp.left_shift(jnp.int32(1), 31 - i)
        cnt = count(lambda c, key: jnp.where(key >= cand, one, zero_i))
        ok = cnt >= topk
        return jnp.where(ok, cand, lo), jnp.where(ok, cnt, cnt_lo)

    lo, cnt_lo = lax.fori_loop(
        0, 32, bit_body,
        (jnp.full((1, tq), INT32_MIN, jnp.int32), jnp.full((1, tq), 1, jnp.int32) * (nch * tq)))
    few = lo <= KEY_NEG_INF
    thr = jnp.where(few, KEY_NEG_INF + 1, lo)
    excess = jnp.where(few, 0, jnp.where(cnt_lo > topk, 1, 0))

    j_ref[...] = jnp.full((1, tq), seq, jnp.int32)
    key_row = lax.broadcasted_iota(jnp.int32, (tq, tq), 0)

    @pl.when(jnp.max(excess) > 0)
    def _():
        need = topk - count(lambda c, key: jnp.where(key > thr, one, zero_i))

        def idx_body(i, x):
            cand = x + jnp.left_shift(jnp.int32(1), (seq - 1).bit_length() - 1 - i)
            before = count(lambda c, key: jnp.where(
                key == thr, jnp.where(c * tq + key_row < cand, one, zero_i), zero_i))
            return jnp.where(before < need, cand, x)

        x = lax.fori_loop(0, (seq - 1).bit_length(), idx_body, jnp.zeros((1, tq), jnp.int32))
        j_ref[...] = jnp.where(excess > 0, x, seq)

    j_lim = j_ref[...]

    def bias_chunk(c, carry):
        key = key_ref[c]
        tie = jnp.where(c * tq + key_row <= j_lim, 0.0, MASK_BIAS)
        b = jnp.where(key > thr, 0.0, jnp.where(key == thr, tie, MASK_BIAS))
        bias_ref[c] = b.T
        return carry

    lax.fori_loop(0, nch, bias_chunk, 0)

    for g in range(ATT_KV_HEADS):
        hs = slice(g * HEAD_DIM, (g + 1) * HEAD_DIM)
        q3 = jnp.concatenate(
            [q_ref[:, (grp * g + r) * HEAD_DIM:(grp * g + r + 1) * HEAD_DIM] for r in range(grp)], axis=0)
        m_ref[...] = jnp.full(m_ref.shape, MASK_BIAS, F32)
        l_ref[...] = jnp.zeros(l_ref.shape, F32)
        acc_ref[...] = jnp.zeros(acc_ref.shape, F32)

        def kv_chunk(c, carry, hs=hs, q3=q3):
            r0 = pl.multiple_of(c * tq, tq)
            kc = k_ref[pl.ds(r0, tq), hs]
            vc = v_ref[pl.ds(r0, tq), hs]
            s = lax.dot_general(q3, kc, nt_dims, preferred_element_type=F32)
            s = s + jnp.concatenate([bias_ref[c]] * grp, axis=0)
            m_prev = m_ref[...]
            m_new = jnp.maximum(m_prev, jnp.max(s, axis=-1, keepdims=True))
            alpha = jnp.exp(m_prev - m_new)
            p = jnp.exp(s - jnp.concatenate([m_new] * (tq // 128), axis=1))
            l_ref[...] = alpha * l_ref[...] + jnp.sum(p, axis=-1, keepdims=True)
            acc_ref[...] = alpha * acc_ref[...] + jnp.dot(p.astype(vc.dtype), vc, preferred_element_type=F32)
            m_ref[...] = m_new
            return carry

        lax.fori_loop(0, nch, kv_chunk, 0)
        out = acc_ref[...] / l_ref[...]
        for r in range(grp):
            h = grp * g + r
            o_ref[:, h * HEAD_DIM:(h + 1) * HEAD_DIM] = out[r * tq:(r + 1) * tq].astype(o_ref.dtype)


def dsa_attention(q, k, v, iq, ik, iwt, *, batch, seq, tq, topk):
    M = q.shape[0]
    nq = seq // tq
    grp = ATT_HEADS // ATT_KV_HEADS
    qrow = lambda w: pl.BlockSpec((tq, w), lambda b, i: (b * nq + i, 0))
    kvrow = lambda w: pl.BlockSpec((seq, w), lambda b, i: (b, 0))
    return pl.pallas_call(
        functools.partial(_attn_kernel, tq=tq, topk=topk, seq=seq),
        grid=(batch, nq),
        in_specs=[qrow(1536), kvrow(512), kvrow(512), qrow(1024), kvrow(128),
                  pl.BlockSpec((IDX_HEADS, tq), lambda b, i: (0, b * nq + i))],
        out_specs=qrow(1536),
        out_shape=jax.ShapeDtypeStruct((M, 1536), BF16),
        scratch_shapes=[
            pltpu.VMEM((IDX_HEADS // 2, 2 * tq, 128), iq.dtype),
            pltpu.VMEM((nq, tq, tq), jnp.int32),
            pltpu.VMEM((nq, tq, tq), F32),
            pltpu.VMEM((1, tq), jnp.int32),
            pltpu.VMEM((grp * tq, 128), F32),
            pltpu.VMEM((grp * tq, 128), F32),
            pltpu.VMEM((grp * tq, HEAD_DIM), F32),
        ],
        compiler_params=pltpu.CompilerParams(
            dimension_semantics=("parallel", "arbitrary"), vmem_limit_bytes=V7X_VMEM_LIMIT_BYTES),
        name="dsa_attention",
    )(q, k, v, iq, ik, iwt)


def causal_dwconv(x, w, b=None):
    K, C = w.shape
    y = lax.conv_general_dilated(x, w[:, None, :].astype(x.dtype), window_strides=(1,),
                                 padding=[(K - 1, 0)], dimension_numbers=('NWC', 'WIO', 'NWC'),
                                 feature_group_count=C)
    return y if b is None else y + b.astype(x.dtype)


def l2norm(x):
    return x * lax.rsqrt(jnp.sum(jnp.square(x), -1, keepdims=True) + RMS_EPS)


def mamba2_ssd(z, xbc, dt_raw, conv_w, conv_b, A_log, dt_bias, D_skip, norm_w):
    Bsz, S, _ = z.shape
    G, R, P, N, Q = SSM_GROUPS, SSM_HEADS // SSM_GROUPS, SSM_HEAD_DIM, SSM_STATE, SSM_CHUNK
    nc = S // Q
    xbc = jax.nn.silu(causal_dwconv(xbc, conv_w, conv_b)).astype(F32)
    xs, Bm, Cm = jnp.split(xbc, [SSM_D_INNER, SSM_D_INNER + G * N], axis=-1)
    dt = jax.nn.softplus(dt_raw.astype(F32) + dt_bias.astype(F32))
    A = -jnp.exp(A_log.astype(F32))
    X = xs.reshape(Bsz, nc, Q, G, R, P)
    dtc = dt.reshape(Bsz, nc, Q, G, R)
    Xdt = X * dtc[..., None]
    Bc = Bm.reshape(Bsz, nc, Q, G, N)
    Cc = Cm.reshape(Bsz, nc, Q, G, N)
    a_cs = jnp.cumsum(dtc * A.reshape(G, R), axis=2)
    causal = jnp.tril(jnp.ones((Q, Q), dtype=bool))[:, :, None, None]
    seg = a_cs[:, :, :, None] - a_cs[:, :, None, :]
    Lmat = jnp.exp(jnp.where(causal, seg, -jnp.inf))
    CB = jnp.einsum('bclgn,bcsgn->bclsg', Cc, Bc)
    y_diag = jnp.einsum('bclsgr,bcsgrp->bclgrp', CB[..., None] * Lmat, Xdt)
    states = jnp.einsum('bclgn,bclgr,bclgrp->bcgrpn', Bc, jnp.exp(a_cs[:, :, -1:] - a_cs), Xdt)
    chunk_decay = jnp.exp(a_cs[:, :, -1])

    def carry_state(h, inp):
        st, dec = inp
        return h * dec[..., None, None] + st, h

    h0 = jnp.zeros((Bsz, G, R, P, N), F32)
    _, h_prev = lax.scan(carry_state, h0, (jnp.moveaxis(states, 1, 0), jnp.moveaxis(chunk_decay, 1, 0)))
    h_prev = jnp.moveaxis(h_prev, 0, 1)
    y_off = jnp.einsum('bclgn,bcgrpn,bclgr->bclgrp', Cc, h_prev, jnp.exp(a_cs))
    y = y_diag + y_off + X * D_skip.astype(F32).reshape(G, R, 1)
    y = y.reshape(Bsz, S, SSM_D_INNER) * jax.nn.silu(z.astype(F32))
    yg = y.reshape(Bsz, S, G, SSM_D_INNER // G)
    yg = yg * lax.rsqrt(jnp.mean(jnp.square(yg), -1, keepdims=True) + RMS_EPS)
    return (yg.reshape(Bsz, S, SSM_D_INNER) * norm_w).astype(z.dtype)


def gated_deltanet(qkv, z, a_raw, b_raw, conv_w, A_log, dt_bias, norm_w):
    Bsz, S, _ = qkv.shape
    H, DK, DV, C = GDN_HEADS, GDN_DK, GDN_DV, GDN_CHUNK
    nc = S // C
    qkv = jax.nn.silu(causal_dwconv(qkv, conv_w)).astype(F32)
    q, k, v = jnp.split(qkv, [H * DK, 2 * H * DK], axis=-1)
    q = l2norm(q.reshape(Bsz, S, H, DK)) * DK ** -0.5
    k = l2norm(k.reshape(Bsz, S, H, DK))
    v = v.reshape(Bsz, S, H, DV)
    beta = jax.nn.sigmoid(b_raw.astype(F32))
    g = -jnp.exp(A_log.astype(F32)) * jax.nn.softplus(a_raw.astype(F32) + dt_bias.astype(F32))

    def chunks(t):
        return jnp.moveaxis(t.reshape(Bsz, nc, C, H, *t.shape[3:]), 3, 1)

    qc, kc, vc, bc = chunks(q), chunks(k), chunks(v), chunks(beta)
    g_cs = jnp.cumsum(chunks(g), axis=-1)
    tril = jnp.tril(jnp.ones((C, C), dtype=bool))
    strict = jnp.tril(jnp.ones((C, C), dtype=bool), -1)
    decay = jnp.exp(jnp.where(tril, g_cs[..., :, None] - g_cs[..., None, :], -jnp.inf))
    kb = kc * bc[..., None]
    lower = jnp.where(strict, jnp.einsum('bhcid,bhcjd->bhcij', kb, kc) * decay, 0.0)
    eye = jnp.eye(C, dtype=F32)
    T = lax.linalg.triangular_solve(lower + eye, jnp.broadcast_to(eye, lower.shape),
                                    left_side=True, lower=True, unit_diagonal=True)
    u = jnp.einsum('bhcij,bhcje->bhcie', T, vc * bc[..., None])
    w = jnp.einsum('bhcij,bhcjd->bhcid', T, kb * jnp.exp(g_cs)[..., None])
    intra = jnp.einsum('bhcid,bhcjd->bhcij', qc, kc) * decay
    q_dec = qc * jnp.exp(g_cs)[..., None]
    k_dec = kc * jnp.exp(g_cs[..., -1:] - g_cs)[..., None]
    chunk_decay = jnp.exp(g_cs[..., -1])

    def step(state, inp):
        qd, kd, u_c, w_c, att, dec = inp
        v_new = u_c - jnp.einsum('bhid,bhde->bhie', w_c, state)
        o = jnp.einsum('bhid,bhde->bhie', qd, state) + jnp.einsum('bhij,bhje->bhie', att, v_new)
        state = state * dec[..., None, None] + jnp.einsum('bhid,bhie->bhde', kd, v_new)
        return state, o

    xs = tuple(jnp.moveaxis(t, 2, 0) for t in (q_dec, k_dec, u, w, intra, chunk_decay))
    _, o = lax.scan(step, jnp.zeros((Bsz, H, DK, DV), F32), xs)
    o = jnp.transpose(o, (1, 0, 3, 2, 4)).reshape(Bsz, S, H, DV)
    o = o * lax.rsqrt(jnp.mean(jnp.square(o), -1, keepdims=True) + RMS_EPS) * norm_w
    o = o * jax.nn.silu(z.astype(F32).reshape(Bsz, S, H, DV))
    return o.reshape(Bsz, S, H * DV).astype(z.dtype)


ATTN_TQ = 256


def _seg_cols(proj, name, batch, seq):
    off, width = SEG[name]
    return proj[:, off:off + width].reshape(batch, seq, width)


def kernel(x, positions, w_in, ssm_conv_w, ssm_conv_b, ssm_A_log, ssm_dt_bias, ssm_D, ssm_norm_w,
           gdn_conv_w, gdn_A_log, gdn_dt_bias, gdn_norm_w, w_gate_up, b_gate, w_branch, w_out,
           ln1_g, ln1_b, w_ffn_in, w_ffn_out, ln2_g, ln2_b):
    B, S, D = x.shape
    M = B * S
    tables = rope_lane_tables(positions)
    topk = min(TOPK_MAX, S // 4)

    xf = x.reshape(M, D)
    xb = xf.astype(BF16)
    for l in range(DEPTH):
        proj = matmul(xb, permute_w_in(w_in[l]), tm=1024, tn=512, out_dtype=F32)
        q, k, v, iq, ik, iwt = attn_prep(proj, tables, tr=256, cdt=BF16)
        y_a = dsa_attention(q, k, v, iq, ik, iwt, batch=B, seq=S, tq=ATTN_TQ, topk=topk)
        seg = functools.partial(_seg_cols, proj, batch=B, seq=S)
        sxbc = jnp.concatenate([seg("sx"), seg("sB"), seg("sC")], axis=-1)
        y_b = mamba2_ssd(seg("sz"), sxbc, seg("sdt")[..., :SSM_HEADS], ssm_conv_w[l], ssm_conv_b[l],
                         ssm_A_log[l], ssm_dt_bias[l], ssm_D[l], ssm_norm_w[l])
        gqkv = jnp.concatenate([seg("gq"), seg("gk"), seg("gv")], axis=-1)
        gab = seg("gab")
        y_c = gated_deltanet(gqkv, seg("gz"), gab[..., :GDN_HEADS], gab[..., GDN_HEADS:2 * GDN_HEADS],
                             gdn_conv_w[l], gdn_A_log[l], gdn_dt_bias[l], gdn_norm_w[l])
        merged = branch_merge(y_a, y_b.reshape(M, -1).astype(BF16), y_c.reshape(M, -1).astype(BF16), proj,
                              w_branch[l].astype(BF16), w_gate_up[l].astype(BF16), b_gate[l],
                              tm=1024, tn=512)
        z = matmul_residual(merged, w_out[l].astype(BF16), xf, alpha=DEEPNORM_ALPHA, tm=1024, tn=512, tk=D)
        xf, xb = layer_norm(z, ln1_g[l], ln1_b[l], tr=256)
        h = swiglu_in(xb, w_ffn_in[l].astype(BF16), tm=1024, tn=256)
        z = matmul_residual(h, w_ffn_out[l].astype(BF16), xf, alpha=DEEPNORM_ALPHA, tm=1024, tn=512,
                            tk=D_FF // 2)
        xf, xb = layer_norm(z, ln2_g[l], ln2_b[l], tr=256)
    return xf.reshape(B, S, D)
```

```python
import functools

import jax
import jax.numpy as jnp
import numpy as np
from jax import lax
from jax.experimental import pallas as pl
from jax.experimental.pallas import tpu as pltpu

F32 = jnp.float32
BF16 = jnp.bfloat16

D_MODEL = 4096
DEPTH = 4
HEAD_DIM = 128
ROPE_THETA = 10000.0
LN_EPS = 1e-5
RMS_EPS = 1e-6
ATT_HEADS = 12
ATT_KV_HEADS = 4
IDX_HEADS = 16
IDX_DIM = 64
TOPK_MAX = 256
Q_BLOCK = 128
SSM_D_INNER = 1536
SSM_HEAD_DIM = 64
SSM_HEADS = SSM_D_INNER // SSM_HEAD_DIM
SSM_GROUPS = 4
SSM_STATE = 128
SSM_CONV = 4
SSM_CHUNK = 128
SSM_CONV_DIM = SSM_D_INNER + 2 * SSM_GROUPS * SSM_STATE
GDN_HEADS = 12
GDN_DK = 128
GDN_DV = 128
GDN_CONV = 4
GDN_CHUNK = 64
GDN_CONV_DIM = GDN_HEADS * (2 * GDN_DK + GDN_DV)
N_BRANCH = 3
BRANCH_WIDTH = 1536
GATE_RANK = 512
D_FF = -(-8 * D_MODEL // (3 * 256)) * 256
DEEPNORM_ALPHA = (2.0 * DEPTH) ** 0.25

IN_SPLITS = (
    ATT_HEADS * HEAD_DIM, ATT_KV_HEADS * HEAD_DIM, ATT_KV_HEADS * HEAD_DIM,
    IDX_HEADS * IDX_DIM, IDX_DIM, IDX_HEADS,
    SSM_D_INNER, SSM_CONV_DIM, SSM_HEADS,
    GDN_CONV_DIM, GDN_HEADS * GDN_DV, GDN_HEADS, GDN_HEADS,
    GATE_RANK,
)
IN_WIDTH = sum(IN_SPLITS)

V7X_VMEM_LIMIT_BYTES = 56 * 1024 * 1024


def _round_up(n, m):
    return -(-n // m) * m


def _mm_kernel(x_ref, w_ref, o_ref):
    o_ref[...] = jnp.dot(x_ref[...], w_ref[...], preferred_element_type=F32).astype(o_ref.dtype)


def matmul(x, w, *, tm, tn, out_dtype):
    M, K = x.shape
    _, N = w.shape
    return pl.pallas_call(
        _mm_kernel,
        grid=(M // tm, N // tn),
        in_specs=[pl.BlockSpec((tm, K), lambda i, j: (i, 0)),
                  pl.BlockSpec((K, tn), lambda i, j: (0, j))],
        out_specs=pl.BlockSpec((tm, tn), lambda i, j: (i, j)),
        out_shape=jax.ShapeDtypeStruct((M, N), out_dtype),
        compiler_params=pltpu.CompilerParams(
            dimension_semantics=("parallel", "arbitrary"),
            vmem_limit_bytes=V7X_VMEM_LIMIT_BYTES),
        name="matmul",
    )(x, w)


def _mm_res_kernel(x_ref, w_ref, r_ref, o_ref, acc_ref, *, alpha, nk):
    k = pl.program_id(2)

    @pl.when(k == 0)
    def _():
        acc_ref[...] = jnp.zeros_like(acc_ref)

    acc_ref[...] += jnp.dot(x_ref[...], w_ref[...].astype(x_ref.dtype), preferred_element_type=F32)

    @pl.when(k == nk - 1)
    def _():
        o_ref[...] = alpha * r_ref[...] + acc_ref[...]


def matmul_residual(x, w, res, *, alpha, tm, tn, tk, layer=None):
    M, K = x.shape
    N = w.shape[-1]
    nk = K // tk
    if layer is None:
        w_spec = pl.BlockSpec((tk, tn), lambda i, j, k: (k, j))
    else:
        w_spec = pl.BlockSpec((None, tk, tn), lambda i, j, k: (layer, k, j))
    return pl.pallas_call(
        functools.partial(_mm_res_kernel, alpha=alpha, nk=nk),
        grid=(M // tm, N // tn, nk),
        in_specs=[pl.BlockSpec((tm, tk), lambda i, j, k: (i, k)),
                  w_spec,
                  pl.BlockSpec((tm, tn), lambda i, j, k: (i, j))],
        out_specs=pl.BlockSpec((tm, tn), lambda i, j, k: (i, j)),
        out_shape=jax.ShapeDtypeStruct((M, N), F32),
        scratch_shapes=[pltpu.VMEM((tm, tn), F32)],
        compiler_params=pltpu.CompilerParams(
            dimension_semantics=("parallel", "arbitrary", "arbitrary"),
            vmem_limit_bytes=V7X_VMEM_LIMIT_BYTES),
        name="matmul_residual",
    )(x, w, res)


def _cast_kernel(w_ref, o_ref):
    o_ref[...] = w_ref[...].astype(o_ref.dtype)


def cast_layer_bf16(w, layer, *, tr):
    _, K, N = w.shape
    return pl.pallas_call(
        _cast_kernel,
        grid=(K // tr,),
        in_specs=[pl.BlockSpec((None, tr, N), lambda i: (layer, i, 0))],
        out_specs=pl.BlockSpec((tr, N), lambda i: (i, 0)),
        out_shape=jax.ShapeDtypeStruct((K, N), BF16),
        compiler_params=pltpu.CompilerParams(
            dimension_semantics=("parallel",), vmem_limit_bytes=V7X_VMEM_LIMIT_BYTES),
        name="cast_layer_bf16",
    )(w)


def _swiglu_kernel(x_ref, wg_ref, wu_ref, o_ref):
    x = x_ref[...]
    g = jnp.dot(x, wg_ref[...].astype(x.dtype), preferred_element_type=F32)
    u = jnp.dot(x, wu_ref[...].astype(x.dtype), preferred_element_type=F32)
    o_ref[...] = (g * jax.nn.sigmoid(g) * u).astype(o_ref.dtype)


def swiglu_in(x, w_in, layer, *, tm, tn):
    M, K = x.shape
    F = w_in.shape[-1] // 2
    nj = F // tn
    return pl.pallas_call(
        _swiglu_kernel,
        grid=(M // tm, nj),
        in_specs=[pl.BlockSpec((tm, K), lambda i, j: (i, 0)),
                  pl.BlockSpec((None, K, tn), lambda i, j: (layer, 0, j)),
                  pl.BlockSpec((None, K, tn), lambda i, j: (layer, 0, j + nj))],
        out_specs=pl.BlockSpec((tm, tn), lambda i, j: (i, j)),
        out_shape=jax.ShapeDtypeStruct((M, F), BF16),
        compiler_params=pltpu.CompilerParams(
            dimension_semantics=("parallel", "arbitrary"),
            vmem_limit_bytes=V7X_VMEM_LIMIT_BYTES),
        name="swiglu_in",
    )(x, w_in, w_in)


def _merge_kernel(ya_ref, yb_ref, yc_ref, gl_ref, wb_ref, wg0_ref, wg1_ref, wg2_ref, bg_ref, o_ref):
    gl = gl_ref[...].astype(BF16)
    acc = None
    for i, (y_ref, wg_ref) in enumerate(((ya_ref, wg0_ref), (yb_ref, wg1_ref), (yc_ref, wg2_ref))):
        gate = jax.nn.sigmoid(jnp.dot(gl, wg_ref[...].astype(BF16), preferred_element_type=F32) + bg_ref[i])
        term = gate * jnp.dot(y_ref[...], wb_ref[i].astype(BF16), preferred_element_type=F32)
        acc = term if acc is None else acc + term
    o_ref[...] = acc.astype(o_ref.dtype)


def branch_merge(ya, yb, yc, proj, w_branch, w_gate_up, b_gate, layer, *, tm, tn):
    M = ya.shape[0]
    L, _, _, D = w_branch.shape
    nj = D // tn
    glat_block = SEG["glat"][0] // GATE_RANK
    ysp = pl.BlockSpec((tm, BRANCH_WIDTH), lambda i, j: (i, 0))
    wg_specs = [pl.BlockSpec((None, GATE_RANK, tn),
                             functools.partial(lambda i, j, b: (layer, 0, j + b * nj), b=b))
                for b in range(N_BRANCH)]
    return pl.pallas_call(
        _merge_kernel,
        grid=(M // tm, nj),
        in_specs=[ysp, ysp, ysp,
                  pl.BlockSpec((tm, GATE_RANK), lambda i, j: (i, glat_block)),
                  pl.BlockSpec((None, N_BRANCH, BRANCH_WIDTH, tn), lambda i, j: (layer, 0, 0, j)),
                  *wg_specs,
                  pl.BlockSpec((None, N_BRANCH, 1, tn), lambda i, j: (layer, 0, 0, j))],
        out_specs=pl.BlockSpec((tm, tn), lambda i, j: (i, j)),
        out_shape=jax.ShapeDtypeStruct((M, D), BF16),
        compiler_params=pltpu.CompilerParams(
            dimension_semantics=("parallel", "arbitrary"),
            vmem_limit_bytes=V7X_VMEM_LIMIT_BYTES),
        name="branch_merge",
    )(ya, yb, yc, proj, w_branch, w_gate_up, w_gate_up, w_gate_up, b_gate.reshape(L, N_BRANCH, 1, D))


def _ln_kernel(z_ref, g_ref, b_ref, o_ref, obf_ref):
    z = z_ref[...]
    mu = jnp.mean(z, -1, keepdims=True)
    zc = z - mu
    var = jnp.mean(zc * zc, -1, keepdims=True)
    y = zc * lax.rsqrt(var + LN_EPS) * g_ref[...] + b_ref[...]
    o_ref[...] = y
    obf_ref[...] = y.astype(BF16)


def layer_norm(z, g, b, *, tr):
    M, D = z.shape
    return pl.pallas_call(
        _ln_kernel,
        grid=(M // tr,),
        in_specs=[pl.BlockSpec((tr, D), lambda i: (i, 0)),
                  pl.BlockSpec((1, D), lambda i: (0, 0)),
                  pl.BlockSpec((1, D), lambda i: (0, 0))],
        out_specs=[pl.BlockSpec((tr, D), lambda i: (i, 0)),
                   pl.BlockSpec((tr, D), lambda i: (i, 0))],
        out_shape=[jax.ShapeDtypeStruct((M, D), F32), jax.ShapeDtypeStruct((M, D), BF16)],
        compiler_params=pltpu.CompilerParams(
            dimension_semantics=("parallel",),
            vmem_limit_bytes=V7X_VMEM_LIMIT_BYTES),
        name="layer_norm",
    )(z, g.reshape(1, D), b.reshape(1, D))


def rope_tables(positions, dim):
    inv = ROPE_THETA ** (-jnp.arange(0, dim, 2, dtype=F32) / dim)
    ang = positions.astype(F32)[..., None] * inv
    return jnp.cos(ang), jnp.sin(ang)


_LAYOUT_ORDER = (
    ("aq", 1536), ("sz", 1536), ("gz", 1536), ("gq", 1536), ("gk", 1536), ("gv", 1536), ("sx", 1536),
    ("ak", 512), ("av", 512), ("sB", 512), ("sC", 512), ("glat", 512),
    ("iq", 1024), ("ik", 128), ("iw", 128), ("sdt", 128), ("gab", 128),
)


def _build_layout():
    segs, off = {}, 0
    for name, width in _LAYOUT_ORDER:
        assert off % width == 0, name
        segs[name] = (off, width)
        off += width
    return segs, off


SEG, IN_WIDTH_PADDED = _build_layout()


def _col_block(name):
    off, width = SEG[name]
    return off // width


def permute_w_in(w, dtype=BF16):
    D = w.shape[0]
    o = [int(v) for v in np.cumsum((0,) + IN_SPLITS)]
    aq, ak, av, iq, ik, iw, sz, sxbc, sdt, gqkv, gz, ga, gb, glat = [w[:, o[i]:o[i + 1]] for i in range(14)]
    half = IDX_DIM // 2
    iq_p = iq.reshape(D, IDX_HEADS // 2, 2, 2, half).transpose(0, 1, 3, 2, 4).reshape(D, IDX_HEADS * IDX_DIM)
    ik_p = jnp.repeat(ik.reshape(D, 2, 1, half), 2, axis=2).reshape(D, 2 * IDX_DIM)

    def pad128(a):
        return jnp.pad(a, ((0, 0), (0, 128 - a.shape[1])))

    G, N = SSM_GROUPS, SSM_STATE
    pieces = {
        "aq": aq, "sz": sz, "gz": gz,
        "gq": gqkv[:, :GDN_HEADS * GDN_DK], "gk": gqkv[:, GDN_HEADS * GDN_DK:2 * GDN_HEADS * GDN_DK],
        "gv": gqkv[:, 2 * GDN_HEADS * GDN_DK:],
        "sx": sxbc[:, :SSM_D_INNER], "ak": ak, "av": av,
        "sB": sxbc[:, SSM_D_INNER:SSM_D_INNER + G * N], "sC": sxbc[:, SSM_D_INNER + G * N:],
        "glat": glat, "iq": iq_p, "ik": ik_p, "iw": pad128(iw), "sdt": pad128(sdt),
        "gab": pad128(jnp.concatenate([ga, gb], axis=1)),
    }
    return jnp.concatenate([pieces[name].astype(dtype) for name, _ in _LAYOUT_ORDER], axis=1)


def rope_lane_tables(positions):
    M = positions.size
    cos, sin = rope_tables(positions.reshape(M), HEAD_DIM)
    cos_i, sin_i = rope_tables(positions.reshape(M), IDX_DIM)
    return (jnp.concatenate([cos, cos], -1), jnp.concatenate([-sin, sin], -1),
            jnp.concatenate([cos_i] * 4, -1), jnp.concatenate([-sin_i, -sin_i, sin_i, sin_i], -1))


def _attn_prep_kernel(aq_ref, ak_ref, av_ref, iq_ref, ik_ref, iw_ref, c_ref, s_ref, ci_ref, si_ref,
                      q_out, k_out, v_out, iq_out, ik_out, iwt_out):
    c, s, ci, si = c_ref[...], s_ref[...], ci_ref[...], si_ref[...]

    def rope(x, cc, ss):
        return x * cc + pltpu.roll(x, 64, 1) * ss

    for h in range(ATT_HEADS):
        sl = slice(h * HEAD_DIM, (h + 1) * HEAD_DIM)
        q_out[:, sl] = (rope(aq_ref[:, sl], c, s) * HEAD_DIM ** -0.5).astype(q_out.dtype)
    for h in range(ATT_KV_HEADS):
        sl = slice(h * HEAD_DIM, (h + 1) * HEAD_DIM)
        k_out[:, sl] = rope(ak_ref[:, sl], c, s).astype(k_out.dtype)
    v_out[...] = av_ref[...].astype(v_out.dtype)
    for p in range(IDX_HEADS // 2):
        sl = slice(p * 128, (p + 1) * 128)
        iq_out[:, sl] = (rope(iq_ref[:, sl], ci, si) * IDX_DIM ** -0.5).astype(iq_out.dtype)
    ik_out[...] = rope(ik_ref[...], ci, si).astype(ik_out.dtype)
    iwt_out[...] = (iw_ref[...] * IDX_HEADS ** -0.5).T[:IDX_HEADS, :]


def attn_prep(proj, tables, *, tr, cdt):
    M = proj.shape[0]

    def seg(name):
        return pl.BlockSpec((tr, SEG[name][1]), functools.partial(lambda i, b: (i, b), b=_col_block(name)))

    tab = pl.BlockSpec((tr, 128), lambda i: (i, 0))
    row = lambda w: pl.BlockSpec((tr, w), lambda i: (i, 0))
    return pl.pallas_call(
        _attn_prep_kernel,
        grid=(M // tr,),
        in_specs=[seg("aq"), seg("ak"), seg("av"), seg("iq"), seg("ik"), seg("iw"), tab, tab, tab, tab],
        out_specs=[row(1536), row(512), row(512), row(1024), row(128),
                   pl.BlockSpec((IDX_HEADS, tr), lambda i: (0, i))],
        out_shape=[jax.ShapeDtypeStruct((M, 1536), cdt), jax.ShapeDtypeStruct((M, 512), cdt),
                   jax.ShapeDtypeStruct((M, 512), cdt), jax.ShapeDtypeStruct((M, 1024), cdt),
                   jax.ShapeDtypeStruct((M, 128), cdt), jax.ShapeDtypeStruct((IDX_HEADS, M), F32)],
        compiler_params=pltpu.CompilerParams(
            dimension_semantics=("parallel",), vmem_limit_bytes=V7X_VMEM_LIMIT_BYTES),
        name="attn_prep",
    )(proj, proj, proj, proj, proj, proj, *tables)


MASK_BIAS = -1e30
INT32_MIN = -2 ** 31
KEY_NEG_INF = int(np.array(-np.inf, np.float32).view(np.int32)) ^ 0x7FFFFFFF
KEY_SUB = 128


def _attn_kernel(q_ref, k_ref, v_ref, iq_ref, ik_ref, iwt_ref, o_ref,
                 iq2_ref, key_ref, bias_ref, j_ref, m_ref, l_ref, acc_ref, *, tq, topk, seq):
    qi = pl.program_id(1)
    nch = qi + 1
    n_pair = IDX_HEADS // 2
    grp = ATT_HEADS // ATT_KV_HEADS
    nt_dims = (((1,), (1,)), ((), ()))

    lane = lax.broadcasted_iota(jnp.int32, (tq, 128), 1)
    is_a = (lane & (IDX_DIM // 2)) == 0
    for p in range(n_pair):
        x = iq_ref[:, p * 128:(p + 1) * 128]
        zero = jnp.zeros_like(x)
        iq2_ref[p, :tq, :] = jnp.where(is_a, x, zero)
        iq2_ref[p, tq:, :] = jnp.where(is_a, zero, x)

    iwt = iwt_ref[...]
    q_pos = qi * tq + lax.broadcasted_iota(jnp.int32, (KEY_SUB, tq), 1)
    sub_iota = lax.broadcasted_iota(jnp.int32, (KEY_SUB, tq), 0)

    def score_chunk(c, carry):
        for sub in range(tq // KEY_SUB):
            r0 = pl.multiple_of(c * tq + sub * KEY_SUB, KEY_SUB)
            ikc = ik_ref[pl.ds(r0, KEY_SUB), :]
            acc = jnp.zeros((KEY_SUB, tq), F32)
            for p in range(n_pair):
                lt = lax.dot_general(ikc, iq2_ref[p], nt_dims, preferred_element_type=F32)
                acc = acc + iwt[2 * p:2 * p + 1, :] * jnp.maximum(lt[:, :tq], 0.0)
                acc = acc + iwt[2 * p + 1:2 * p + 2, :] * jnp.maximum(lt[:, tq:], 0.0)
            acc = jnp.where(r0 + sub_iota <= q_pos, acc, -jnp.inf)
            bits = lax.bitcast_convert_type(acc, jnp.int32)
            key_ref[c, sub * KEY_SUB:(sub + 1) * KEY_SUB, :] = bits ^ ((bits >> 31) & 0x7FFFFFFF)
        return carry

    lax.fori_loop(0, nch, score_chunk, 0)

    one = jnp.ones((tq, tq), jnp.int32)
    zero_i = jnp.zeros((tq, tq), jnp.int32)

    def count(indicator):
        def body(c, cnt8):
            ind = indicator(c, key_ref[c])
            return cnt8 + ind.reshape(tq // 8, 8, tq).sum(axis=0)
        cnt8 = lax.fori_loop(0, nch, body, jnp.zeros((8, tq), jnp.int32))
        return cnt8.sum(axis=0, keepdims=True)

    def bit_body(i, carry):
        lo, cnt_lo = carry
        cand = lo + jnp.left_shift(jnp.int32(1), 31 - i)
        cnt = count(lambda c, key: jnp.where(key >= cand, one, zero_i))
        ok = cnt >= topk
        return jnp.where(ok, cand, lo), jnp.where(ok, cnt, cnt_lo)

    lo, cnt_lo = lax.fori_loop(
        0, 32, bit_body,
        (jnp.full((1, tq), INT32_MIN, jnp.int32), jnp.full((1, tq), 1, jnp.int32) * (nch * tq)))
    few = lo <= KEY_NEG_INF
    thr = jnp.where(few, KEY_NEG_INF + 1, lo)
    excess = jnp.where(few, 0, jnp.where(cnt_lo > topk, 1, 0))

    j_ref[...] = jnp.full((1, tq), seq, jnp.int32)
    key_row = lax.broadcasted_iota(jnp.int32, (tq, tq), 0)

    @pl.when(jnp.max(excess) > 0)
    def _():
        need = topk - count(lambda c, key: jnp.where(key > thr, one, zero_i))

        def idx_body(i, x):
            cand = x + jnp.left_shift(jnp.int32(1), (seq - 1).bit_length() - 1 - i)
            before = count(lambda c, key: jnp.where(
                key == thr, jnp.where(c * tq + key_row < cand, one, zero_i), zero_i))
            return jnp.where(before < need, cand, x)

        x = lax.fori_loop(0, (seq - 1).bit_length(), idx_body, jnp.zeros((1, tq), jnp.int32))
        j_ref[...] = jnp.where(excess > 0, x, seq)

    j_lim = j_ref[...]

    def bias_chunk(c, carry):
        key = key_ref[c]
        tie = jnp.where(c * tq + key_row <= j_lim, 0.0, MASK_BIAS)
        b = jnp.where(key > thr, 0.0, jnp.where(key == thr, tie, MASK_BIAS))
        bias_ref[c] = b.T
        return carry

    lax.fori_loop(0, nch, bias_chunk, 0)

    for g in range(ATT_KV_HEADS):
        hs = slice(g * HEAD_DIM, (g + 1) * HEAD_DIM)
        q3 = jnp.concatenate(
            [q_ref[:, (grp * g + r) * HEAD_DIM:(grp * g + r + 1) * HEAD_DIM] for r in range(grp)], axis=0)
        m_ref[...] = jnp.full(m_ref.shape, MASK_BIAS, F32)
        l_ref[...] = jnp.zeros(l_ref.shape, F32)
        acc_ref[...] = jnp.zeros(acc_ref.shape, F32)

        def kv_chunk(c, carry, hs=hs, q3=q3):
            r0 = pl.multiple_of(c * tq, tq)
            kc = k_ref[pl.ds(r0, tq), hs]
            vc = v_ref[pl.ds(r0, tq), hs]
            s = lax.dot_general(q3, kc, nt_dims, preferred_element_type=F32)
            s = s + jnp.concatenate([bias_ref[c]] * grp, axis=0)
            m_prev = m_ref[...]
            m_new = jnp.maximum(m_prev, jnp.max(s, axis=-1, keepdims=True))
            alpha = jnp.exp(m_prev - m_new)
            p = jnp.exp(s - jnp.concatenate([m_new] * (tq // 128), axis=1))
            l_ref[...] = alpha * l_ref[...] + jnp.sum(p, axis=-1, keepdims=True)
            acc_ref[...] = alpha * acc_ref[...] + jnp.dot(p.astype(vc.dtype), vc, preferred_element_type=F32)
            m_ref[...] = m_new
            return carry

        lax.fori_loop(0, nch, kv_chunk, 0)
        out = acc_ref[...] / l_ref[...]
        for r in range(grp):
            h = grp * g + r
            o_ref[:, h * HEAD_DIM:(h + 1) * HEAD_DIM] = out[r * tq:(r + 1) * tq].astype(o_ref.dtype)


def dsa_attention(q, k, v, iq, ik, iwt, *, batch, seq, tq, topk):
    M = q.shape[0]
    nq = seq // tq
    grp = ATT_HEADS // ATT_KV_HEADS
    qrow = lambda w: pl.BlockSpec((tq, w), lambda b, i: (b * nq + i, 0))
    kvrow = lambda w: pl.BlockSpec((seq, w), lambda b, i: (b, 0))
    return pl.pallas_call(
        functools.partial(_attn_kernel, tq=tq, topk=topk, seq=seq),
        grid=(batch, nq),
        in_specs=[qrow(1536), kvrow(512), kvrow(512), qrow(1024), kvrow(128),
                  pl.BlockSpec((IDX_HEADS, tq), lambda b, i: (0, b * nq + i))],
        out_specs=qrow(1536),
        out_shape=jax.ShapeDtypeStruct((M, 1536), BF16),
        scratch_shapes=[
            pltpu.VMEM((IDX_HEADS // 2, 2 * tq, 128), iq.dtype),
            pltpu.VMEM((nq, tq, tq), jnp.int32),
            pltpu.VMEM((nq, tq, tq), F32),
            pltpu.VMEM((1, tq), jnp.int32),
            pltpu.VMEM((grp * tq, 128), F32),
            pltpu.VMEM((grp * tq, 128), F32),
            pltpu.VMEM((grp * tq, HEAD_DIM), F32),
        ],
        compiler_params=pltpu.CompilerParams(
            dimension_semantics=("parallel", "arbitrary"), vmem_limit_bytes=V7X_VMEM_LIMIT_BYTES),
        name="dsa_attention",
    )(q, k, v, iq, ik, iwt)


CONV_TAIL = 8


def _dwconv_silu(x, prev, w_ref, bias):
    taps = w_ref.shape[0]
    row = lax.broadcasted_iota(jnp.int32, prev.shape, 0)
    y = x * w_ref[taps - 1:taps, :]
    if bias is not None:
        y = y + bias
    for j in range(1, taps):
        xs = pltpu.roll(x, j, 0)
        head = jnp.where(row < j, pltpu.roll(prev, j, 0), xs[:CONV_TAIL])
        xs = jnp.concatenate([head, xs[CONV_TAIL:]], axis=0)
        y = y + xs * w_ref[taps - 1 - j:taps - j, :]
    return y * jax.nn.sigmoid(y)


def _softplus(x):
    return jnp.maximum(x, 0.0) + jnp.log(1.0 + jnp.exp(-jnp.abs(x)))


def _cumsum_rows(mask_bf16, v):
    hi = v.astype(BF16)
    r1 = v - hi.astype(F32)
    mid = r1.astype(BF16)
    lo = (r1 - mid.astype(F32)).astype(BF16)
    dot = lambda p: jnp.dot(mask_bf16, p, preferred_element_type=F32)
    return dot(hi) + dot(mid) + dot(lo)


_NT = (((1,), (1,)), ((), ()))


def _ssd_kernel(z_ref, x_ref, b_ref, c_ref, dt_ref, cwx_ref, cwb_ref, cwc_ref, cbx_ref, cbb_ref, cbc_ref,
                arow_ref, dtb_ref, dfull_ref, nw_ref, o_ref, tx_ref, tb_ref, tc_ref, st_ref, *, q, cdt):
    @pl.when(pl.program_id(1) == 0)
    def _():
        tx_ref[...] = jnp.zeros_like(tx_ref)
        tb_ref[...] = jnp.zeros_like(tb_ref)
        tc_ref[...] = jnp.zeros_like(tc_ref)
        st_ref[...] = jnp.zeros_like(st_ref)

    def conv(in_ref, tail_ref, w_ref, bias_ref):
        x = in_ref[...]
        y = _dwconv_silu(x, tail_ref[...], w_ref, bias_ref[...])
        tail_ref[...] = x[q - CONV_TAIL:]
        return y

    xs = conv(x_ref, tx_ref, cwx_ref, cbx_ref)
    bm = conv(b_ref, tb_ref, cwb_ref, cbb_ref)
    cm = conv(c_ref, tc_ref, cwc_ref, cbc_ref)

    P, N = SSM_HEAD_DIM, SSM_STATE
    pairs_per_group = SSM_HEADS // SSM_GROUPS // 2
    row = lax.broadcasted_iota(jnp.int32, (q, q), 0)
    col = lax.broadcasted_iota(jnp.int32, (q, q), 1)
    tril = row >= col
    lane = lax.broadcasted_iota(jnp.int32, (q, 2 * P), 1)
    lo_half = lane < P
    srow_lo = lax.broadcasted_iota(jnp.int32, (2 * P, N), 0) < P

    dt = _softplus(dt_ref[...] + dtb_ref[...])
    a_cs = _cumsum_rows(jnp.where(tril, 1.0, 0.0).astype(BF16), dt * arow_ref[...])
    a_last = a_cs[q - 1:q, :]
    e_cs = jnp.exp(a_cs)
    e_last = jnp.exp(a_last)
    w_state = dt * jnp.exp(a_last - a_cs)
    a_t = a_cs.T
    dt_t = dt.T

    ys = []
    for g in range(SSM_GROUPS):
        bg = bm[:, g * N:(g + 1) * N]
        cg = cm[:, g * N:(g + 1) * N].astype(cdt)
        cb = lax.dot_general(cg, bg.astype(cdt), _NT, preferred_element_type=F32)
        for j in range(pairs_per_group):
            t = g * pairs_per_group + j
            h0, h1 = 2 * t, 2 * t + 1
            xp = xs[:, t * 2 * P:(t + 1) * 2 * P]
            xp_c = xp.astype(cdt)
            diag = []
            for h in (h0, h1):
                seg = a_cs[:, h:h + 1] - a_t[h:h + 1, :]
                m = cb * jnp.exp(jnp.where(tril, seg, -jnp.inf)) * dt_t[h:h + 1, :]
                diag.append(jnp.dot(m.astype(cdt), xp_c, preferred_element_type=F32))
            st = st_ref[t]
            y_off = lax.dot_general(cg, st.astype(cdt), _NT, preferred_element_type=F32)
            e_pair = jnp.where(lo_half, e_cs[:, h0:h0 + 1], e_cs[:, h1:h1 + 1])
            ys.append(jnp.where(lo_half, diag[0], diag[1]) + y_off * e_pair
                      + xp * dfull_ref[:, t * 2 * P:(t + 1) * 2 * P])
            xp_t = xp.T.astype(cdt)
            z0 = jnp.dot(xp_t, (bg * w_state[:, h0:h0 + 1]).astype(cdt), preferred_element_type=F32)
            z1 = jnp.dot(xp_t, (bg * w_state[:, h1:h1 + 1]).astype(cdt), preferred_element_type=F32)
            dec = jnp.where(srow_lo, e_last[:, h0:h0 + 1], e_last[:, h1:h1 + 1])
            st_ref[t] = st * dec + jnp.where(srow_lo, z0, z1)

    gw = SSM_D_INNER // SSM_GROUPS
    tiles = gw // (2 * P)
    for g in range(SSM_GROUPS):
        yg = jnp.concatenate(ys[g * tiles:(g + 1) * tiles], axis=1)
        zg = z_ref[:, g * gw:(g + 1) * gw]
        yg = yg * (zg * jax.nn.sigmoid(zg))
        yg = yg * lax.rsqrt(jnp.mean(yg * yg, -1, keepdims=True) + RMS_EPS)
        o_ref[:, g * gw:(g + 1) * gw] = (yg * nw_ref[:, g * gw:(g + 1) * gw]).astype(o_ref.dtype)


def _pad_lanes(v, offset=0):
    return jnp.zeros((1, 128), F32).at[0, offset:offset + v.shape[0]].set(v.astype(F32))


def ssd_mixer(proj, conv_w, conv_b, A_log, dt_bias, D_skip, norm_w, *, batch, seq, cdt=BF16):
    M = proj.shape[0]
    q = SSM_CHUNK
    nc = seq // q
    GN = SSM_GROUPS * SSM_STATE
    n_pairs = SSM_HEADS // 2

    def seg(name):
        return pl.BlockSpec((q, SEG[name][1]),
                            functools.partial(lambda b, c, blk: (b * nc + c, blk), blk=_col_block(name)))

    full = lambda a: pl.BlockSpec(a.shape, lambda b, c: (0,) * a.ndim)
    params = [conv_w[:, :SSM_D_INNER], conv_w[:, SSM_D_INNER:SSM_D_INNER + GN], conv_w[:, SSM_D_INNER + GN:],
              conv_b[None, :SSM_D_INNER], conv_b[None, SSM_D_INNER:SSM_D_INNER + GN],
              conv_b[None, SSM_D_INNER + GN:],
              _pad_lanes(-jnp.exp(A_log.astype(F32))), _pad_lanes(dt_bias),
              jnp.repeat(D_skip.astype(F32), SSM_HEAD_DIM)[None, :], norm_w[None, :].astype(F32)]
    return pl.pallas_call(
        functools.partial(_ssd_kernel, q=q, cdt=cdt),
        grid=(batch, nc),
        in_specs=[seg("sz"), seg("sx"), seg("sB"), seg("sC"), seg("sdt")] + [full(a) for a in params],
        out_specs=pl.BlockSpec((q, SSM_D_INNER), lambda b, c: (b * nc + c, 0)),
        out_shape=jax.ShapeDtypeStruct((M, SSM_D_INNER), BF16),
        scratch_shapes=[pltpu.VMEM((CONV_TAIL, SSM_D_INNER), F32), pltpu.VMEM((CONV_TAIL, GN), F32),
                        pltpu.VMEM((CONV_TAIL, GN), F32),
                        pltpu.VMEM((n_pairs, 2 * SSM_HEAD_DIM, SSM_STATE), F32)],
        compiler_params=pltpu.CompilerParams(
            dimension_semantics=("parallel", "arbitrary"), vmem_limit_bytes=V7X_VMEM_LIMIT_BYTES),
        name="ssd_mixer",
    )(proj, proj, proj, proj, proj, *params)


GDN_TILE = 2 * GDN_CHUNK


def _gdn_kernel(q_ref, k_ref, v_ref, z_ref, ab_ref, cwq_ref, cwk_ref, cwv_ref, arow_ref, dtb_ref, nw_ref,
                o_ref, tq_ref, tk_ref, tv_ref, st_ref, *, cdt):
    T, C = GDN_TILE, GDN_CHUNK

    @pl.when(pl.program_id(1) == 0)
    def _():
        tq_ref[...] = jnp.zeros_like(tq_ref)
        tk_ref[...] = jnp.zeros_like(tk_ref)
        tv_ref[...] = jnp.zeros_like(tv_ref)
        st_ref[...] = jnp.zeros_like(st_ref)

    def conv(in_ref, tail_ref, w_ref):
        x = in_ref[...]
        y = _dwconv_silu(x, tail_ref[...], w_ref, None)
        tail_ref[...] = x[T - CONV_TAIL:]
        return y

    qc = conv(q_ref, tq_ref, cwq_ref)
    kc = conv(k_ref, tk_ref, cwk_ref)
    vc = conv(v_ref, tv_ref, cwv_ref)

    row = lax.broadcasted_iota(jnp.int32, (T, T), 0)
    col = lax.broadcasted_iota(jnp.int32, (T, T), 1)
    same = (row // C) == (col // C)
    tril = jnp.logical_and(same, row >= col)
    strict = jnp.logical_and(same, row > col)
    eye = jnp.where(row == col, 1.0, 0.0)
    row_w = lax.broadcasted_iota(jnp.int32, (T, 128), 0)

    ab = ab_ref[...]
    g = arow_ref[...] * _softplus(ab + dtb_ref[...])
    beta_all = jax.nn.sigmoid(ab)
    g_cs = _cumsum_rows(jnp.where(tril, 1.0, 0.0).astype(BF16), g)
    g_last = jnp.where(row_w < C, g_cs[C - 1:C, :], g_cs[T - 1:T, :])
    e_cs = jnp.exp(g_cs)
    e_rem = jnp.exp(g_last - g_cs)
    g_t = g_cs.T
    chunk_rows = [row_w < C, row_w >= C]

    def mm(a, b):
        return jnp.dot(a.astype(cdt), b.astype(cdt), preferred_element_type=F32)

    heads = range(GDN_HEADS)
    hsl = [slice(h * GDN_DK, (h + 1) * GDN_DK) for h in heads]
    col_of = lambda a, h: a[:, h:h + 1]
    kn = [kc[:, hsl[h]] * lax.rsqrt(jnp.sum(kc[:, hsl[h]] * kc[:, hsl[h]], -1, keepdims=True) + RMS_EPS)
          for h in heads]
    qn = [qc[:, hsl[h]] * lax.rsqrt(jnp.sum(qc[:, hsl[h]] * qc[:, hsl[h]], -1, keepdims=True) + RMS_EPS)
          * GDN_DK ** -0.5 for h in heads]
    beta = [col_of(beta_all, GDN_HEADS + h) for h in heads]
    kb = [kn[h] * beta[h] for h in heads]
    kn_c = [kn[h].astype(cdt) for h in heads]
    decay = [jnp.exp(jnp.where(tril, col_of(g_cs, h) - g_t[h:h + 1, :], -jnp.inf)) for h in heads]
    kk = [lax.dot_general(kb[h].astype(cdt), kn_c[h], _NT, preferred_element_type=F32) for h in heads]
    qk = [lax.dot_general(qn[h].astype(cdt), kn_c[h], _NT, preferred_element_type=F32) for h in heads]
    y = [-jnp.where(strict, kk[h] * decay[h], 0.0) for h in heads]
    p = [eye + y[h] for h in heads]
    for _ in range((C - 1).bit_length() - 1):
        y = [mm(y[h], y[h]) for h in heads]
        p = [p[h] + mm(p[h], y[h]) for h in heads]
    u = [mm(p[h], vc[:, hsl[h]] * beta[h]) for h in heads]
    w = [mm(p[h], kb[h] * col_of(e_cs, h)) for h in heads]
    intra = [qk[h] * decay[h] for h in heads]
    q_dec = [qn[h] * col_of(e_cs, h) for h in heads]
    kd_t = [(kn[h] * col_of(e_rem, h)).T for h in heads]
    outs = [[] for _ in heads]
    for c in range(T // C):
        rs = slice(c * C, (c + 1) * C)
        s = [st_ref[h] for h in heads]
        v_new = [u[h][rs] - mm(w[h][rs], s[h]) for h in heads]
        v_pad = [jnp.where(chunk_rows[c], jnp.concatenate([v_new[h]] * (T // C), axis=0), 0.0) for h in heads]
        for h in heads:
            outs[h].append(mm(q_dec[h][rs], s[h]) + mm(intra[h][rs], v_pad[h]))
        for h in heads:
            dec = jnp.exp(g_cs[(c + 1) * C - 1:(c + 1) * C, h:h + 1])
            st_ref[h] = s[h] * dec + mm(kd_t[h], v_pad[h])
    for h in heads:
        o = jnp.concatenate(outs[h], axis=0)
        o = o * lax.rsqrt(jnp.mean(o * o, -1, keepdims=True) + RMS_EPS) * nw_ref[...]
        zh = z_ref[:, hsl[h]]
        o_ref[:, hsl[h]] = (o * (zh * jax.nn.sigmoid(zh))).astype(o_ref.dtype)


def gdn_mixer(proj, conv_w, A_log, dt_bias, norm_w, *, batch, seq, cdt=BF16):
    M = proj.shape[0]
    T = GDN_TILE
    nt = seq // T
    W = GDN_HEADS * GDN_DK

    def seg(name):
        return pl.BlockSpec((T, SEG[name][1]),
                            functools.partial(lambda b, c, blk: (b * nt + c, blk), blk=_col_block(name)))

    full = lambda a: pl.BlockSpec(a.shape, lambda b, c: (0,) * a.ndim)
    params = [conv_w[:, :W], conv_w[:, W:2 * W], conv_w[:, 2 * W:],
              _pad_lanes(-jnp.exp(A_log.astype(F32))), _pad_lanes(dt_bias), norm_w[None, :].astype(F32)]
    return pl.pallas_call(
        functools.partial(_gdn_kernel, cdt=cdt),
        grid=(batch, nt),
        in_specs=[seg("gq"), seg("gk"), seg("gv"), seg("gz"), seg("gab")] + [full(a) for a in params],
        out_specs=pl.BlockSpec((T, GDN_HEADS * GDN_DV), lambda b, c: (b * nt + c, 0)),
        out_shape=jax.ShapeDtypeStruct((M, GDN_HEADS * GDN_DV), BF16),
        scratch_shapes=[pltpu.VMEM((CONV_TAIL, W), F32), pltpu.VMEM((CONV_TAIL, W), F32),
                        pltpu.VMEM((CONV_TAIL, W), F32),
                        pltpu.VMEM((GDN_HEADS, GDN_DK, GDN_DV), F32)],
        compiler_params=pltpu.CompilerParams(
            dimension_semantics=("parallel", "arbitrary"), vmem_limit_bytes=V7X_VMEM_LIMIT_BYTES),
        name="gdn_mixer",
    )(proj, proj, proj, proj, proj, *params)


def causal_dwconv(x, w, b=None):
    K, C = w.shape
    y = lax.conv_general_dilated(x, w[:, None, :].astype(x.dtype), window_strides=(1,),
                                 padding=[(K - 1, 0)], dimension_numbers=('NWC', 'WIO', 'NWC'),
                                 feature_group_count=C)
    return y if b is None else y + b.astype(x.dtype)


def l2norm(x):
    return x * lax.rsqrt(jnp.sum(jnp.square(x), -1, keepdims=True) + RMS_EPS)


def mamba2_ssd(z, xbc, dt_raw, conv_w, conv_b, A_log, dt_bias, D_skip, norm_w):
    Bsz, S, _ = z.shape
    G, R, P, N, Q = SSM_GROUPS, SSM_HEADS // SSM_GROUPS, SSM_HEAD_DIM, SSM_STATE, SSM_CHUNK
    nc = S // Q
    xbc = jax.nn.silu(causal_dwconv(xbc, conv_w, conv_b)).astype(F32)
    xs, Bm, Cm = jnp.split(xbc, [SSM_D_INNER, SSM_D_INNER + G * N], axis=-1)
    dt = jax.nn.softplus(dt_raw.astype(F32) + dt_bias.astype(F32))
    A = -jnp.exp(A_log.astype(F32))
    X = xs.reshape(Bsz, nc, Q, G, R, P)
    dtc = dt.reshape(Bsz, nc, Q, G, R)
    Xdt = X * dtc[..., None]
    Bc = Bm.reshape(Bsz, nc, Q, G, N)
    Cc = Cm.reshape(Bsz, nc, Q, G, N)
    a_cs = jnp.cumsum(dtc * A.reshape(G, R), axis=2)
    causal = jnp.tril(jnp.ones((Q, Q), dtype=bool))[:, :, None, None]
    seg = a_cs[:, :, :, None] - a_cs[:, :, None, :]
    Lmat = jnp.exp(jnp.where(causal, seg, -jnp.inf))
    CB = jnp.einsum('bclgn,bcsgn->bclsg', Cc, Bc)
    y_diag = jnp.einsum('bclsgr,bcsgrp->bclgrp', CB[..., None] * Lmat, Xdt)
    states = jnp.einsum('bclgn,bclgr,bclgrp->bcgrpn', Bc, jnp.exp(a_cs[:, :, -1:] - a_cs), Xdt)
    chunk_decay = jnp.exp(a_cs[:, :, -1])

    def carry_state(h, inp):
        st, dec = inp
        return h * dec[..., None, None] + st, h

    h0 = jnp.zeros((Bsz, G, R, P, N), F32)
    _, h_prev = lax.scan(carry_state, h0, (jnp.moveaxis(states, 1, 0), jnp.moveaxis(chunk_decay, 1, 0)))
    h_prev = jnp.moveaxis(h_prev, 0, 1)
    y_off = jnp.einsum('bclgn,bcgrpn,bclgr->bclgrp', Cc, h_prev, jnp.exp(a_cs))
    y = y_diag + y_off + X * D_skip.astype(F32).reshape(G, R, 1)
    y = y.reshape(Bsz, S, SSM_D_INNER) * jax.nn.silu(z.astype(F32))
    yg = y.reshape(Bsz, S, G, SSM_D_INNER // G)
    yg = yg * lax.rsqrt(jnp.mean(jnp.square(yg), -1, keepdims=True) + RMS_EPS)
    return (yg.reshape(Bsz, S, SSM_D_INNER) * norm_w).astype(z.dtype)


def gated_deltanet(qkv, z, a_raw, b_raw, conv_w, A_log, dt_bias, norm_w):
    Bsz, S, _ = qkv.shape
    H, DK, DV, C = GDN_HEADS, GDN_DK, GDN_DV, GDN_CHUNK
    nc = S // C
    qkv = jax.nn.silu(causal_dwconv(qkv, conv_w)).astype(F32)
    q, k, v = jnp.split(qkv, [H * DK, 2 * H * DK], axis=-1)
    q = l2norm(q.reshape(Bsz, S, H, DK)) * DK ** -0.5
    k = l2norm(k.reshape(Bsz, S, H, DK))
    v = v.reshape(Bsz, S, H, DV)
    beta = jax.nn.sigmoid(b_raw.astype(F32))
    g = -jnp.exp(A_log.astype(F32)) * jax.nn.softplus(a_raw.astype(F32) + dt_bias.astype(F32))

    def chunks(t):
        return jnp.moveaxis(t.reshape(Bsz, nc, C, H, *t.shape[3:]), 3, 1)

    qc, kc, vc, bc = chunks(q), chunks(k), chunks(v), chunks(beta)
    g_cs = jnp.cumsum(chunks(g), axis=-1)
    tril = jnp.tril(jnp.ones((C, C), dtype=bool))
    strict = jnp.tril(jnp.ones((C, C), dtype=bool), -1)
    decay = jnp.exp(jnp.where(tril, g_cs[..., :, None] - g_cs[..., None, :], -jnp.inf))
    kb = kc * bc[..., None]
    lower = jnp.where(strict, jnp.einsum('bhcid,bhcjd->bhcij', kb, kc) * decay, 0.0)
    eye = jnp.eye(C, dtype=F32)
    T = lax.linalg.triangular_solve(lower + eye, jnp.broadcast_to(eye, lower.shape),
                                    left_side=True, lower=True, unit_diagonal=True)
    u = jnp.einsum('bhcij,bhcje->bhcie', T, vc * bc[..., None])
    w = jnp.einsum('bhcij,bhcjd->bhcid', T, kb * jnp.exp(g_cs)[..., None])
    intra = jnp.einsum('bhcid,bhcjd->bhcij', qc, kc) * decay
    q_dec = qc * jnp.exp(g_cs)[..., None]
    k_dec = kc * jnp.exp(g_cs[..., -1:] - g_cs)[..., None]
    chunk_decay = jnp.exp(g_cs[..., -1])

    def step(state, inp):
        qd, kd, u_c, w_c, att, dec = inp
        v_new = u_c - jnp.einsum('bhid,bhde->bhie', w_c, state)
        o = jnp.einsum('bhid,bhde->bhie', qd, state) + jnp.einsum('bhij,bhje->bhie', att, v_new)
        state = state * dec[..., None, None] + jnp.einsum('bhid,bhie->bhde', kd, v_new)
        return state, o

    xs = tuple(jnp.moveaxis(t, 2, 0) for t in (q_dec, k_dec, u, w, intra, chunk_decay))
    _, o = lax.scan(step, jnp.zeros((Bsz, H, DK, DV), F32), xs)
    o = jnp.transpose(o, (1, 0, 3, 2, 4)).reshape(Bsz, S, H, DV)
    o = o * lax.rsqrt(jnp.mean(jnp.square(o), -1, keepdims=True) + RMS_EPS) * norm_w
    o = o * jax.nn.silu(z.astype(F32).reshape(Bsz, S, H, DV))
    return o.reshape(Bsz, S, H * DV).astype(z.dtype)


ATTN_TQ = 256
DENSE_TM = 1024
ROW_TILE = 256


def _seg_cols(proj, name, batch, seq):
    off, width = SEG[name]
    return proj[:, off:off + width].reshape(batch, seq, width)


def kernel(x, positions, w_in, ssm_conv_w, ssm_conv_b, ssm_A_log, ssm_dt_bias, ssm_D, ssm_norm_w,
           gdn_conv_w, gdn_A_log, gdn_dt_bias, gdn_norm_w, w_gate_up, b_gate, w_branch, w_out,
           ln1_g, ln1_b, w_ffn_in, w_ffn_out, ln2_g, ln2_b):
    B, S, D = x.shape
    M = B * S
    tables = rope_lane_tables(positions)
    topk = min(TOPK_MAX, S // 4)

    tm = min(DENSE_TM, M)
    xf = x.reshape(M, D)
    xb = xf.astype(BF16)
    for l in range(DEPTH):
        proj = matmul(xb, permute_w_in(w_in[l]), tm=tm, tn=512, out_dtype=F32)
        q, k, v, iq, ik, iwt = attn_prep(proj, tables, tr=ROW_TILE, cdt=BF16)
        y_a = dsa_attention(q, k, v, iq, ik, iwt, batch=B, seq=S, tq=ATTN_TQ, topk=topk)
        y_b = ssd_mixer(proj, ssm_conv_w[l], ssm_conv_b[l], ssm_A_log[l], ssm_dt_bias[l], ssm_D[l],
                        ssm_norm_w[l], batch=B, seq=S)
        y_c = gdn_mixer(proj, gdn_conv_w[l], gdn_A_log[l], gdn_dt_bias[l], gdn_norm_w[l], batch=B, seq=S)
        merged = branch_merge(y_a, y_b, y_c, proj, w_branch, w_gate_up, b_gate, l, tm=tm, tn=256)
        z = matmul_residual(merged, w_out, xf, alpha=DEEPNORM_ALPHA, tm=tm, tn=512, tk=D, layer=l)
        xf, xb = layer_norm(z, ln1_g[l], ln1_b[l], tr=ROW_TILE)
        h = swiglu_in(xb, w_ffn_in, l, tm=tm, tn=256)
        z = matmul_residual(h, cast_layer_bf16(w_ffn_out, l, tr=ROW_TILE), xf, alpha=DEEPNORM_ALPHA,
                            tm=tm, tn=512, tk=D_FF // 2)
        xf, xb = layer_norm(z, ln2_g[l], ln2_b[l], tr=ROW_TILE)
    return xf.reshape(B, S, D)
```

```python
import functools

import jax
import jax.numpy as jnp
import numpy as np
from jax import lax
from jax.experimental import pallas as pl
from jax.experimental.pallas import tpu as pltpu

F32 = jnp.float32
BF16 = jnp.bfloat16

D_MODEL = 4096
DEPTH = 4
HEAD_DIM = 128
ROPE_THETA = 10000.0
LN_EPS = 1e-5
RMS_EPS = 1e-6
ATT_HEADS = 12
ATT_KV_HEADS = 4
IDX_HEADS = 16
IDX_DIM = 64
TOPK_MAX = 256
Q_BLOCK = 128
SSM_D_INNER = 1536
SSM_HEAD_DIM = 64
SSM_HEADS = SSM_D_INNER // SSM_HEAD_DIM
SSM_GROUPS = 4
SSM_STATE = 128
SSM_CONV = 4
SSM_CHUNK = 128
SSM_CONV_DIM = SSM_D_INNER + 2 * SSM_GROUPS * SSM_STATE
GDN_HEADS = 12
GDN_DK = 128
GDN_DV = 128
GDN_CONV = 4
GDN_CHUNK = 64
GDN_CONV_DIM = GDN_HEADS * (2 * GDN_DK + GDN_DV)
N_BRANCH = 3
BRANCH_WIDTH = 1536
GATE_RANK = 512
D_FF = -(-8 * D_MODEL // (3 * 256)) * 256
DEEPNORM_ALPHA = (2.0 * DEPTH) ** 0.25

IN_SPLITS = (
    ATT_HEADS * HEAD_DIM, ATT_KV_HEADS * HEAD_DIM, ATT_KV_HEADS * HEAD_DIM,
    IDX_HEADS * IDX_DIM, IDX_DIM, IDX_HEADS,
    SSM_D_INNER, SSM_CONV_DIM, SSM_HEADS,
    GDN_CONV_DIM, GDN_HEADS * GDN_DV, GDN_HEADS, GDN_HEADS,
    GATE_RANK,
)
IN_WIDTH = sum(IN_SPLITS)

V7X_VMEM_LIMIT_BYTES = 56 * 1024 * 1024


def _round_up(n, m):
    return -(-n // m) * m


def _mm_kernel(x_ref, w_ref, o_ref):
    o_ref[...] = jnp.dot(x_ref[...], w_ref[...], preferred_element_type=F32).astype(o_ref.dtype)


def matmul(x, w, *, tm, tn, out_dtype):
    M, K = x.shape
    _, N = w.shape
    return pl.pallas_call(
        _mm_kernel,
        grid=(M // tm, N // tn),
        in_specs=[pl.BlockSpec((tm, K), lambda i, j: (i, 0)),
                  pl.BlockSpec((K, tn), lambda i, j: (0, j))],
        out_specs=pl.BlockSpec((tm, tn), lambda i, j: (i, j)),
        out_shape=jax.ShapeDtypeStruct((M, N), out_dtype),
        compiler_params=pltpu.CompilerParams(
            dimension_semantics=("parallel", "arbitrary"),
            vmem_limit_bytes=V7X_VMEM_LIMIT_BYTES),
        name="matmul",
    )(x, w)


def _mm_res_kernel(x_ref, w_ref, r_ref, o_ref, acc_ref, *, alpha, nk):
    k = pl.program_id(2)

    @pl.when(k == 0)
    def _():
        acc_ref[...] = jnp.zeros_like(acc_ref)

    acc_ref[...] += jnp.dot(x_ref[...], w_ref[...].astype(x_ref.dtype), preferred_element_type=F32)

    @pl.when(k == nk - 1)
    def _():
        o_ref[...] = alpha * r_ref[...] + acc_ref[...]


def matmul_residual(x, w, res, *, alpha, tm, tn, tk, layer=None):
    M, K = x.shape
    N = w.shape[-1]
    nk = K // tk
    if layer is None:
        w_spec = pl.BlockSpec((tk, tn), lambda i, j, k: (k, j))
    else:
        w_spec = pl.BlockSpec((None, tk, tn), lambda i, j, k: (layer, k, j))
    return pl.pallas_call(
        functools.partial(_mm_res_kernel, alpha=alpha, nk=nk),
        grid=(M // tm, N // tn, nk),
        in_specs=[pl.BlockSpec((tm, tk), lambda i, j, k: (i, k)),
                  w_spec,
                  pl.BlockSpec((tm, tn), lambda i, j, k: (i, j))],
        out_specs=pl.BlockSpec((tm, tn), lambda i, j, k: (i, j)),
        out_shape=jax.ShapeDtypeStruct((M, N), F32),
        scratch_shapes=[pltpu.VMEM((tm, tn), F32)],
        compiler_params=pltpu.CompilerParams(
            dimension_semantics=("parallel", "arbitrary", "arbitrary"),
            vmem_limit_bytes=V7X_VMEM_LIMIT_BYTES),
        name="matmul_residual",
    )(x, w, res)


def _cast_kernel(w_ref, o_ref):
    o_ref[...] = w_ref[...].astype(o_ref.dtype)


def cast_layer_bf16(w, layer, *, tr):
    _, K, N = w.shape
    return pl.pallas_call(
        _cast_kernel,
        grid=(K // tr,),
        in_specs=[pl.BlockSpec((None, tr, N), lambda i: (layer, i, 0))],
        out_specs=pl.BlockSpec((tr, N), lambda i: (i, 0)),
        out_shape=jax.ShapeDtypeStruct((K, N), BF16),
        compiler_params=pltpu.CompilerParams(
            dimension_semantics=("parallel",), vmem_limit_bytes=V7X_VMEM_LIMIT_BYTES),
        name="cast_layer_bf16",
    )(w)


def _swiglu_kernel(x_ref, wg_ref, wu_ref, o_ref):
    x = x_ref[...]
    g = jnp.dot(x, wg_ref[...].astype(x.dtype), preferred_element_type=F32)
    u = jnp.dot(x, wu_ref[...].astype(x.dtype), preferred_element_type=F32)
    o_ref[...] = (g * jax.nn.sigmoid(g) * u).astype(o_ref.dtype)


def swiglu_in(x, w_in, layer, *, tm, tn):
    M, K = x.shape
    F = w_in.shape[-1] // 2
    nj = F // tn
    return pl.pallas_call(
        _swiglu_kernel,
        grid=(M // tm, nj),
        in_specs=[pl.BlockSpec((tm, K), lambda i, j: (i, 0)),
                  pl.BlockSpec((None, K, tn), lambda i, j: (layer, 0, j)),
                  pl.BlockSpec((None, K, tn), lambda i, j: (layer, 0, j + nj))],
        out_specs=pl.BlockSpec((tm, tn), lambda i, j: (i, j)),
        out_shape=jax.ShapeDtypeStruct((M, F), BF16),
        compiler_params=pltpu.CompilerParams(
            dimension_semantics=("parallel", "arbitrary"),
            vmem_limit_bytes=V7X_VMEM_LIMIT_BYTES),
        name="swiglu_in",
    )(x, w_in, w_in)


def _merge_kernel(ya_ref, yb_ref, yc_ref, gl_ref, wb_ref, wg0_ref, wg1_ref, wg2_ref, bg_ref, o_ref):
    gl = gl_ref[...].astype(BF16)
    acc = None
    for i, (y_ref, wg_ref) in enumerate(((ya_ref, wg0_ref), (yb_ref, wg1_ref), (yc_ref, wg2_ref))):
        gate = jax.nn.sigmoid(jnp.dot(gl, wg_ref[...].astype(BF16), preferred_element_type=F32) + bg_ref[i])
        term = gate * jnp.dot(y_ref[...], wb_ref[i].astype(BF16), preferred_element_type=F32)
        acc = term if acc is None else acc + term
    o_ref[...] = acc.astype(o_ref.dtype)


def branch_merge(ya, yb, yc, proj, w_branch, w_gate_up, b_gate, layer, *, tm, tn):
    M = ya.shape[0]
    L, _, _, D = w_branch.shape
    nj = D // tn
    glat_block = SEG["glat"][0] // GATE_RANK
    ysp = pl.BlockSpec((tm, BRANCH_WIDTH), lambda i, j: (i, 0))
    wg_specs = [pl.BlockSpec((None, GATE_RANK, tn),
                             functools.partial(lambda i, j, b: (layer, 0, j + b * nj), b=b))
                for b in range(N_BRANCH)]
    return pl.pallas_call(
        _merge_kernel,
        grid=(M // tm, nj),
        in_specs=[ysp, ysp, ysp,
                  pl.BlockSpec((tm, GATE_RANK), lambda i, j: (i, glat_block)),
                  pl.BlockSpec((None, N_BRANCH, BRANCH_WIDTH, tn), lambda i, j: (layer, 0, 0, j)),
                  *wg_specs,
                  pl.BlockSpec((None, N_BRANCH, 1, tn), lambda i, j: (layer, 0, 0, j))],
        out_specs=pl.BlockSpec((tm, tn), lambda i, j: (i, j)),
        out_shape=jax.ShapeDtypeStruct((M, D), BF16),
        compiler_params=pltpu.CompilerParams(
            dimension_semantics=("parallel", "arbitrary"),
            vmem_limit_bytes=V7X_VMEM_LIMIT_BYTES),
        name="branch_merge",
    )(ya, yb, yc, proj, w_branch, w_gate_up, w_gate_up, w_gate_up, b_gate.reshape(L, N_BRANCH, 1, D))


def _ln_kernel(z_ref, g_ref, b_ref, o_ref, obf_ref):
    z = z_ref[...]
    mu = jnp.mean(z, -1, keepdims=True)
    zc = z - mu
    var = jnp.mean(zc * zc, -1, keepdims=True)
    y = zc * lax.rsqrt(var + LN_EPS) * g_ref[...] + b_ref[...]
    o_ref[...] = y
    obf_ref[...] = y.astype(BF16)


def layer_norm(z, g, b, *, tr):
    M, D = z.shape
    return pl.pallas_call(
        _ln_kernel,
        grid=(M // tr,),
        in_specs=[pl.BlockSpec((tr, D), lambda i: (i, 0)),
                  pl.BlockSpec((1, D), lambda i: (0, 0)),
                  pl.BlockSpec((1, D), lambda i: (0, 0))],
        out_specs=[pl.BlockSpec((tr, D), lambda i: (i, 0)),
                   pl.BlockSpec((tr, D), lambda i: (i, 0))],
        out_shape=[jax.ShapeDtypeStruct((M, D), F32), jax.ShapeDtypeStruct((M, D), BF16)],
        compiler_params=pltpu.CompilerParams(
            dimension_semantics=("parallel",),
            vmem_limit_bytes=V7X_VMEM_LIMIT_BYTES),
        name="layer_norm",
    )(z, g.reshape(1, D), b.reshape(1, D))


def rope_tables(positions, dim):
    inv = ROPE_THETA ** (-jnp.arange(0, dim, 2, dtype=F32) / dim)
    ang = positions.astype(F32)[..., None] * inv
    return jnp.cos(ang), jnp.sin(ang)


_LAYOUT_ORDER = (
    ("aq", 1536), ("sz", 1536), ("gz", 1536), ("gq", 1536), ("gk", 1536), ("gv", 1536), ("sx", 1536),
    ("ak", 512), ("av", 512), ("sB", 512), ("sC", 512), ("glat", 512),
    ("iq", 1024), ("ik", 128), ("iw", 128), ("sdt", 128), ("gab", 128),
)


def _build_layout():
    segs, off = {}, 0
    for name, width in _LAYOUT_ORDER:
        assert off % width == 0, name
        segs[name] = (off, width)
        off += width
    return segs, off


SEG, IN_WIDTH_PADDED = _build_layout()


def _col_block(name):
    off, width = SEG[name]
    return off // width


def permute_w_in(w, dtype=BF16):
    D = w.shape[0]
    o = [int(v) for v in np.cumsum((0,) + IN_SPLITS)]
    aq, ak, av, iq, ik, iw, sz, sxbc, sdt, gqkv, gz, ga, gb, glat = [w[:, o[i]:o[i + 1]] for i in range(14)]
    half = IDX_DIM // 2
    iq_p = iq.reshape(D, IDX_HEADS // 2, 2, 2, half).transpose(0, 1, 3, 2, 4).reshape(D, IDX_HEADS * IDX_DIM)
    ik_p = jnp.repeat(ik.reshape(D, 2, 1, half), 2, axis=2).reshape(D, 2 * IDX_DIM)

    def pad128(a):
        return jnp.pad(a, ((0, 0), (0, 128 - a.shape[1])))

    G, N = SSM_GROUPS, SSM_STATE
    pieces = {
        "aq": aq, "sz": sz, "gz": gz,
        "gq": gqkv[:, :GDN_HEADS * GDN_DK], "gk": gqkv[:, GDN_HEADS * GDN_DK:2 * GDN_HEADS * GDN_DK],
        "gv": gqkv[:, 2 * GDN_HEADS * GDN_DK:],
        "sx": sxbc[:, :SSM_D_INNER], "ak": ak, "av": av,
        "sB": sxbc[:, SSM_D_INNER:SSM_D_INNER + G * N], "sC": sxbc[:, SSM_D_INNER + G * N:],
        "glat": glat, "iq": iq_p, "ik": ik_p, "iw": pad128(iw), "sdt": pad128(sdt),
        "gab": pad128(jnp.concatenate([ga, gb], axis=1)),
    }
    return jnp.concatenate([pieces[name].astype(dtype) for name, _ in _LAYOUT_ORDER], axis=1)


def rope_lane_tables(positions):
    M = positions.size
    cos, sin = rope_tables(positions.reshape(M), HEAD_DIM)
    cos_i, sin_i = rope_tables(positions.reshape(M), IDX_DIM)
    return (jnp.concatenate([cos, cos], -1), jnp.concatenate([-sin, sin], -1),
            jnp.concatenate([cos_i] * 4, -1), jnp.concatenate([-sin_i, -sin_i, sin_i, sin_i], -1))


def _attn_prep_kernel(aq_ref, ak_ref, av_ref, iq_ref, ik_ref, iw_ref, c_ref, s_ref, ci_ref, si_ref,
                      q_out, k_out, v_out, iq_out, ik_out, iwt_out):
    c, s, ci, si = c_ref[...], s_ref[...], ci_ref[...], si_ref[...]

    def rope(x, cc, ss):
        return x * cc + pltpu.roll(x, 64, 1) * ss

    for h in range(ATT_HEADS):
        sl = slice(h * HEAD_DIM, (h + 1) * HEAD_DIM)
        q_out[:, sl] = (rope(aq_ref[:, sl], c, s) * HEAD_DIM ** -0.5).astype(q_out.dtype)
    for h in range(ATT_KV_HEADS):
        sl = slice(h * HEAD_DIM, (h + 1) * HEAD_DIM)
        k_out[:, sl] = rope(ak_ref[:, sl], c, s).astype(k_out.dtype)
    v_out[...] = av_ref[...].astype(v_out.dtype)
    for p in range(IDX_HEADS // 2):
        sl = slice(p * 128, (p + 1) * 128)
        iq_out[:, sl] = (rope(iq_ref[:, sl], ci, si) * IDX_DIM ** -0.5).astype(iq_out.dtype)
    ik_out[...] = rope(ik_ref[...], ci, si).astype(ik_out.dtype)
    iwt_out[...] = (iw_ref[...] * IDX_HEADS ** -0.5).T[:IDX_HEADS, :]


def attn_prep(proj, tables, *, tr, cdt):
    M = proj.shape[0]

    def seg(name):
        return pl.BlockSpec((tr, SEG[name][1]), functools.partial(lambda i, b: (i, b), b=_col_block(name)))

    tab = pl.BlockSpec((tr, 128), lambda i: (i, 0))
    row = lambda w: pl.BlockSpec((tr, w), lambda i: (i, 0))
    return pl.pallas_call(
        _attn_prep_kernel,
        grid=(M // tr,),
        in_specs=[seg("aq"), seg("ak"), seg("av"), seg("iq"), seg("ik"), seg("iw"), tab, tab, tab, tab],
        out_specs=[row(1536), row(512), row(512), row(1024), row(128),
                   pl.BlockSpec((IDX_HEADS, tr), lambda i: (0, i))],
        out_shape=[jax.ShapeDtypeStruct((M, 1536), cdt), jax.ShapeDtypeStruct((M, 512), cdt),
                   jax.ShapeDtypeStruct((M, 512), cdt), jax.ShapeDtypeStruct((M, 1024), cdt),
                   jax.ShapeDtypeStruct((M, 128), cdt), jax.ShapeDtypeStruct((IDX_HEADS, M), F32)],
        compiler_params=pltpu.CompilerParams(
            dimension_semantics=("parallel",), vmem_limit_bytes=V7X_VMEM_LIMIT_BYTES),
        name="attn_prep",
    )(proj, proj, proj, proj, proj, proj, *tables)


MASK_BIAS = -1e30
INT32_MIN = -2 ** 31
KEY_NEG_INF = int(np.array(-np.inf, np.float32).view(np.int32)) ^ 0x7FFFFFFF
KEY_SUB = 128
SOFTMAX_ROWS = 64


def _attn_kernel(q_ref, k_ref, v_ref, iq_ref, ik_ref, iwt_ref, o_ref,
                 iq2_ref, key_ref, bias_ref, j_ref, q3_ref, p_ref, m_ref, l_ref, acc_ref, *, tq, topk, seq):
    qi = pl.program_id(1)
    nch = qi + 1
    n_pair = IDX_HEADS // 2
    grp = ATT_HEADS // ATT_KV_HEADS
    nt_dims = (((1,), (1,)), ((), ()))

    lane = lax.broadcasted_iota(jnp.int32, (tq, 128), 1)
    is_a = (lane & (IDX_DIM // 2)) == 0
    for p in range(n_pair):
        x = iq_ref[:, p * 128:(p + 1) * 128]
        zero = jnp.zeros_like(x)
        iq2_ref[p, :tq, :] = jnp.where(is_a, x, zero)
        iq2_ref[p, tq:, :] = jnp.where(is_a, zero, x)

    iwt = iwt_ref[...]
    q_pos = qi * tq + lax.broadcasted_iota(jnp.int32, (KEY_SUB, tq), 1)
    sub_iota = lax.broadcasted_iota(jnp.int32, (KEY_SUB, tq), 0)

    def score_chunk(c, carry):
        for sub in range(tq // KEY_SUB):
            r0 = pl.multiple_of(c * tq + sub * KEY_SUB, KEY_SUB)
            ikc = ik_ref[pl.ds(r0, KEY_SUB), :]
            acc = jnp.zeros((KEY_SUB, tq), F32)
            for p in range(n_pair):
                lt = lax.dot_general(ikc, iq2_ref[p], nt_dims, preferred_element_type=F32)
                acc = acc + iwt[2 * p:2 * p + 1, :] * jnp.maximum(lt[:, :tq], 0.0)
                acc = acc + iwt[2 * p + 1:2 * p + 2, :] * jnp.maximum(lt[:, tq:], 0.0)
            acc = jnp.where(r0 + sub_iota <= q_pos, acc, -jnp.inf)
            bits = lax.bitcast_convert_type(acc, jnp.int32)
            key_ref[c, sub * KEY_SUB:(sub + 1) * KEY_SUB, :] = bits ^ ((bits >> 31) & 0x7FFFFFFF)
        return carry

    lax.fori_loop(0, nch, score_chunk, 0)

    one = jnp.ones((tq, tq), jnp.int32)
    zero_i = jnp.zeros((tq, tq), jnp.int32)

    def count(indicator):
        def body(c, cnt8):
            ind = indicator(c, key_ref[c])
            return cnt8 + ind.reshape(4, tq // 32, 8, tq).sum(axis=1).sum(axis=0)
        cnt8 = lax.fori_loop(0, nch, body, jnp.zeros((8, tq), jnp.int32))
        return cnt8.sum(axis=0, keepdims=True)

    def bit_body(i, carry):
        lo, cnt_lo = carry
        cand = lo + jnp.left_shift(jnp.int32(1), 31 - i)
        cnt = count(lambda c, key: jnp.where(key >= cand, one, zero_i))
        ok = cnt >= topk
        return jnp.where(ok, cand, lo), jnp.where(ok, cnt, cnt_lo)

    lo, cnt_lo = lax.fori_loop(
        0, 32, bit_body,
        (jnp.full((1, tq), INT32_MIN, jnp.int32), jnp.full((1, tq), 1, jnp.int32) * (nch * tq)))
    few = lo <= KEY_NEG_INF
    thr = jnp.where(few, KEY_NEG_INF + 1, lo)
    excess = jnp.where(few, 0, jnp.where(cnt_lo > topk, 1, 0))

    j_ref[...] = jnp.full((1, tq), seq, jnp.int32)
    key_row = lax.broadcasted_iota(jnp.int32, (tq, tq), 0)

    @pl.when(jnp.max(excess) > 0)
    def _():
        need = topk - count(lambda c, key: jnp.where(key > thr, one, zero_i))

        def idx_body(i, x):
            cand = x + jnp.left_shift(jnp.int32(1), (seq - 1).bit_length() - 1 - i)
            before = count(lambda c, key: jnp.where(
                key == thr, jnp.where(c * tq + key_row < cand, one, zero_i), zero_i))
            return jnp.where(before < need, cand, x)

        x = lax.fori_loop(0, (seq - 1).bit_length(), idx_body, jnp.zeros((1, tq), jnp.int32))
        j_ref[...] = jnp.where(excess > 0, x, seq)

    j_lim = j_ref[...]

    def bias_chunk(c, carry):
        key = key_ref[c]
        tie = jnp.where(c * tq + key_row <= j_lim, 0.0, MASK_BIAS)
        b = jnp.where(key > thr, 0.0, jnp.where(key == thr, tie, MASK_BIAS))
        bias_ref[c] = b.T
        return carry

    lax.fori_loop(0, nch, bias_chunk, 0)

    groups = range(ATT_KV_HEADS)
    hsl = [slice(g * HEAD_DIM, (g + 1) * HEAD_DIM) for g in groups]
    for g in groups:
        for r in range(grp):
            h = grp * g + r
            q3_ref[g, r * tq:(r + 1) * tq, :] = q_ref[:, h * HEAD_DIM:(h + 1) * HEAD_DIM]
    m_ref[...] = jnp.full(m_ref.shape, MASK_BIAS, F32)
    l_ref[...] = jnp.zeros(l_ref.shape, F32)
    acc_ref[...] = jnp.zeros(acc_ref.shape, F32)

    def kv_chunk(c, carry):
        r0 = pl.multiple_of(c * tq, tq)

        def scores(g):
            return lax.dot_general(q3_ref[g], k_ref[pl.ds(r0, tq), hsl[g]], nt_dims,
                                   preferred_element_type=F32)

        s_next = scores(0)
        for g in groups:
            s = s_next
            if g + 1 < ATT_KV_HEADS:
                s_next = scores(g + 1)
            vc = v_ref[pl.ds(r0, tq), hsl[g]]
            for rb in range(grp * tq // SOFTMAX_ROWS):
                rows = slice(rb * SOFTMAX_ROWS, (rb + 1) * SOFTMAX_ROWS)
                b0 = (rb * SOFTMAX_ROWS) % tq
                s_rb = s[rows] + bias_ref[c, b0:b0 + SOFTMAX_ROWS, :]
                m_prev = m_ref[g, rows, :]
                m_new = jnp.maximum(m_prev, jnp.max(s_rb, axis=-1, keepdims=True))
                alpha = jnp.exp(m_prev - m_new)
                p = jnp.exp(s_rb - jnp.concatenate([m_new] * (tq // 128), axis=1))
                l_ref[g, rows, :] = alpha * l_ref[g, rows, :] + jnp.sum(p, axis=-1, keepdims=True)
                acc_ref[g, rows, :] = alpha * acc_ref[g, rows, :]
                m_ref[g, rows, :] = m_new
                p_ref[rows, :] = p.astype(p_ref.dtype)
            acc_ref[g] += jnp.dot(p_ref[...], vc, preferred_element_type=F32)
        return carry

    lax.fori_loop(0, nch, kv_chunk, 0)
    for g in groups:
        out = acc_ref[g] / l_ref[g]
        for r in range(grp):
            h = grp * g + r
            o_ref[:, h * HEAD_DIM:(h + 1) * HEAD_DIM] = out[r * tq:(r + 1) * tq].astype(o_ref.dtype)


def dsa_attention(q, k, v, iq, ik, iwt, *, batch, seq, tq, topk):
    M = q.shape[0]
    nq = seq // tq
    grp = ATT_HEADS // ATT_KV_HEADS
    qrow = lambda w: pl.BlockSpec((tq, w), lambda b, i: (b * nq + i, 0))
    kvrow = lambda w: pl.BlockSpec((seq, w), lambda b, i: (b, 0))
    return pl.pallas_call(
        functools.partial(_attn_kernel, tq=tq, topk=topk, seq=seq),
        grid=(batch, nq),
        in_specs=[qrow(1536), kvrow(512), kvrow(512), qrow(1024), kvrow(128),
                  pl.BlockSpec((IDX_HEADS, tq), lambda b, i: (0, b * nq + i))],
        out_specs=qrow(1536),
        out_shape=jax.ShapeDtypeStruct((M, 1536), BF16),
        scratch_shapes=[
            pltpu.VMEM((IDX_HEADS // 2, 2 * tq, 128), iq.dtype),
            pltpu.VMEM((nq, tq, tq), jnp.int32),
            pltpu.VMEM((nq, tq, tq), F32),
            pltpu.VMEM((1, tq), jnp.int32),
            pltpu.VMEM((ATT_KV_HEADS, grp * tq, HEAD_DIM), q.dtype),
            pltpu.VMEM((grp * tq, tq), q.dtype),
            pltpu.VMEM((ATT_KV_HEADS, grp * tq, 128), F32),
            pltpu.VMEM((ATT_KV_HEADS, grp * tq, 128), F32),
            pltpu.VMEM((ATT_KV_HEADS, grp * tq, HEAD_DIM), F32),
        ],
        compiler_params=pltpu.CompilerParams(
            dimension_semantics=("parallel", "arbitrary"), vmem_limit_bytes=V7X_VMEM_LIMIT_BYTES),
        name="dsa_attention",
    )(q, k, v, iq, ik, iwt)


CONV_TAIL = 8


def _dwconv_silu(x, prev, w_ref, bias):
    taps = w_ref.shape[0]
    row = lax.broadcasted_iota(jnp.int32, prev.shape, 0)
    y = x * w_ref[taps - 1:taps, :]
    if bias is not None:
        y = y + bias
    for j in range(1, taps):
        xs = pltpu.roll(x, j, 0)
        head = jnp.where(row < j, pltpu.roll(prev, j, 0), xs[:CONV_TAIL])
        xs = jnp.concatenate([head, xs[CONV_TAIL:]], axis=0)
        y = y + xs * w_ref[taps - 1 - j:taps - j, :]
    return y * jax.nn.sigmoid(y)


def _softplus(x):
    return jnp.maximum(x, 0.0) + jnp.log(1.0 + jnp.exp(-jnp.abs(x)))


def _cumsum_rows(mask_bf16, v):
    hi = v.astype(BF16)
    r1 = v - hi.astype(F32)
    mid = r1.astype(BF16)
    lo = (r1 - mid.astype(F32)).astype(BF16)
    dot = lambda p: jnp.dot(mask_bf16, p, preferred_element_type=F32)
    return dot(hi) + dot(mid) + dot(lo)


_NT = (((1,), (1,)), ((), ()))


def _ssd_kernel(z_ref, x_ref, b_ref, c_ref, dt_ref, cwx_ref, cwb_ref, cwc_ref, cbx_ref, cbb_ref, cbc_ref,
                arow_ref, dtb_ref, dfull_ref, nw_ref, o_ref, tx_ref, tb_ref, tc_ref, st_ref, *, q, cdt):
    @pl.when(pl.program_id(1) == 0)
    def _():
        tx_ref[...] = jnp.zeros_like(tx_ref)
        tb_ref[...] = jnp.zeros_like(tb_ref)
        tc_ref[...] = jnp.zeros_like(tc_ref)
        st_ref[...] = jnp.zeros_like(st_ref)

    def conv(in_ref, tail_ref, w_ref, bias_ref):
        x = in_ref[...]
        y = _dwconv_silu(x, tail_ref[...], w_ref, bias_ref[...])
        tail_ref[...] = x[q - CONV_TAIL:]
        return y

    xs = conv(x_ref, tx_ref, cwx_ref, cbx_ref)
    bm = conv(b_ref, tb_ref, cwb_ref, cbb_ref)
    cm = conv(c_ref, tc_ref, cwc_ref, cbc_ref)

    P, N = SSM_HEAD_DIM, SSM_STATE
    pairs_per_group = SSM_HEADS // SSM_GROUPS // 2
    row = lax.broadcasted_iota(jnp.int32, (q, q), 0)
    col = lax.broadcasted_iota(jnp.int32, (q, q), 1)
    tril = row >= col
    lane = lax.broadcasted_iota(jnp.int32, (q, 2 * P), 1)
    lo_half = lane < P
    srow_lo = lax.broadcasted_iota(jnp.int32, (2 * P, N), 0) < P

    dt = _softplus(dt_ref[...] + dtb_ref[...])
    a_cs = _cumsum_rows(jnp.where(tril, 1.0, 0.0).astype(BF16), dt * arow_ref[...])
    a_last = a_cs[q - 1:q, :]
    e_cs = jnp.exp(a_cs)
    e_last = jnp.exp(a_last)
    w_state = dt * jnp.exp(a_last - a_cs)
    a_t = a_cs.T
    dt_t = dt.T

    ys = []
    for g in range(SSM_GROUPS):
        bg = bm[:, g * N:(g + 1) * N]
        cg = cm[:, g * N:(g + 1) * N].astype(cdt)
        cb = lax.dot_general(cg, bg.astype(cdt), _NT, preferred_element_type=F32)
        for j in range(pairs_per_group):
            t = g * pairs_per_group + j
            h0, h1 = 2 * t, 2 * t + 1
            xp = xs[:, t * 2 * P:(t + 1) * 2 * P]
            xp_c = xp.astype(cdt)
            diag = []
            for h in (h0, h1):
                seg = a_cs[:, h:h + 1] - a_t[h:h + 1, :]
                m = cb * jnp.exp(jnp.where(tril, seg, -jnp.inf)) * dt_t[h:h + 1, :]
                diag.append(jnp.dot(m.astype(cdt), xp_c, preferred_element_type=F32))
            st = st_ref[t]
            y_off = lax.dot_general(cg, st.astype(cdt), _NT, preferred_element_type=F32)
            e_pair = jnp.where(lo_half, e_cs[:, h0:h0 + 1], e_cs[:, h1:h1 + 1])
            ys.append(jnp.where(lo_half, diag[0], diag[1]) + y_off * e_pair
                      + xp * dfull_ref[:, t * 2 * P:(t + 1) * 2 * P])
            xp_t = xp.T.astype(cdt)
            z0 = jnp.dot(xp_t, (bg * w_state[:, h0:h0 + 1]).astype(cdt), preferred_element_type=F32)
            z1 = jnp.dot(xp_t, (bg * w_state[:, h1:h1 + 1]).astype(cdt), preferred_element_type=F32)
            dec = jnp.where(srow_lo, e_last[:, h0:h0 + 1], e_last[:, h1:h1 + 1])
            st_ref[t] = st * dec + jnp.where(srow_lo, z0, z1)

    gw = SSM_D_INNER // SSM_GROUPS
    tiles = gw // (2 * P)
    for g in range(SSM_GROUPS):
        yg = jnp.concatenate(ys[g * tiles:(g + 1) * tiles], axis=1)
        zg = z_ref[:, g * gw:(g + 1) * gw]
        yg = yg * (zg * jax.nn.sigmoid(zg))
        yg = yg * lax.rsqrt(jnp.mean(yg * yg, -1, keepdims=True) + RMS_EPS)
        o_ref[:, g * gw:(g + 1) * gw] = (yg * nw_ref[:, g * gw:(g + 1) * gw]).astype(o_ref.dtype)


def _pad_lanes(v, offset=0):
    return jnp.zeros((1, 128), F32).at[0, offset:offset + v.shape[0]].set(v.astype(F32))


def ssd_mixer(proj, conv_w, conv_b, A_log, dt_bias, D_skip, norm_w, *, batch, seq, cdt=BF16):
    M = proj.shape[0]
    q = SSM_CHUNK
    nc = seq // q
    GN = SSM_GROUPS * SSM_STATE
    n_pairs = SSM_HEADS // 2

    def seg(name):
        return pl.BlockSpec((q, SEG[name][1]),
                            functools.partial(lambda b, c, blk: (b * nc + c, blk), blk=_col_block(name)))

    full = lambda a: pl.BlockSpec(a.shape, lambda b, c: (0,) * a.ndim)
    params = [conv_w[:, :SSM_D_INNER], conv_w[:, SSM_D_INNER:SSM_D_INNER + GN], conv_w[:, SSM_D_INNER + GN:],
              conv_b[None, :SSM_D_INNER], conv_b[None, SSM_D_INNER:SSM_D_INNER + GN],
              conv_b[None, SSM_D_INNER + GN:],
              _pad_lanes(-jnp.exp(A_log.astype(F32))), _pad_lanes(dt_bias),
              jnp.repeat(D_skip.astype(F32), SSM_HEAD_DIM)[None, :], norm_w[None, :].astype(F32)]
    return pl.pallas_call(
        functools.partial(_ssd_kernel, q=q, cdt=cdt),
        grid=(batch, nc),
        in_specs=[seg("sz"), seg("sx"), seg("sB"), seg("sC"), seg("sdt")] + [full(a) for a in params],
        out_specs=pl.BlockSpec((q, SSM_D_INNER), lambda b, c: (b * nc + c, 0)),
        out_shape=jax.ShapeDtypeStruct((M, SSM_D_INNER), BF16),
        scratch_shapes=[pltpu.VMEM((CONV_TAIL, SSM_D_INNER), F32), pltpu.VMEM((CONV_TAIL, GN), F32),
                        pltpu.VMEM((CONV_TAIL, GN), F32),
                        pltpu.VMEM((n_pairs, 2 * SSM_HEAD_DIM, SSM_STATE), F32)],
        compiler_params=pltpu.CompilerParams(
            dimension_semantics=("parallel", "arbitrary"), vmem_limit_bytes=V7X_VMEM_LIMIT_BYTES),
        name="ssd_mixer",
    )(proj, proj, proj, proj, proj, *params)


GDN_TILE = 2 * GDN_CHUNK


def _gdn_kernel(q_ref, k_ref, v_ref, z_ref, ab_ref, cwq_ref, cwk_ref, cwv_ref, arow_ref, dtb_ref, nw_ref,
                o_ref, tq_ref, tk_ref, tv_ref, st_ref, *, cdt):
    T, C = GDN_TILE, GDN_CHUNK

    @pl.when(pl.program_id(1) == 0)
    def _():
        tq_ref[...] = jnp.zeros_like(tq_ref)
        tk_ref[...] = jnp.zeros_like(tk_ref)
        tv_ref[...] = jnp.zeros_like(tv_ref)
        st_ref[...] = jnp.zeros_like(st_ref)

    def conv(in_ref, tail_ref, w_ref):
        x = in_ref[...]
        y = _dwconv_silu(x, tail_ref[...], w_ref, None)
        tail_ref[...] = x[T - CONV_TAIL:]
        return y

    qc = conv(q_ref, tq_ref, cwq_ref)
    kc = conv(k_ref, tk_ref, cwk_ref)
    vc = conv(v_ref, tv_ref, cwv_ref)

    row = lax.broadcasted_iota(jnp.int32, (T, T), 0)
    col = lax.broadcasted_iota(jnp.int32, (T, T), 1)
    same = (row // C) == (col // C)
    tril = jnp.logical_and(same, row >= col)
    strict = jnp.logical_and(same, row > col)
    eye = jnp.where(row == col, 1.0, 0.0)
    row_w = lax.broadcasted_iota(jnp.int32, (T, 128), 0)

    ab = ab_ref[...]
    g = arow_ref[...] * _softplus(ab + dtb_ref[...])
    beta_all = jax.nn.sigmoid(ab)
    g_cs = _cumsum_rows(jnp.where(tril, 1.0, 0.0).astype(BF16), g)
    g_last = jnp.where(row_w < C, g_cs[C - 1:C, :], g_cs[T - 1:T, :])
    e_cs = jnp.exp(g_cs)
    e_rem = jnp.exp(g_last - g_cs)
    g_t = g_cs.T
    chunk_rows = [row_w < C, row_w >= C]

    def mm(a, b):
        return jnp.dot(a.astype(cdt), b.astype(cdt), preferred_element_type=F32)

    heads = range(GDN_HEADS)
    hsl = [slice(h * GDN_DK, (h + 1) * GDN_DK) for h in heads]
    col_of = lambda a, h: a[:, h:h + 1]
    kn = [kc[:, hsl[h]] * lax.rsqrt(jnp.sum(kc[:, hsl[h]] * kc[:, hsl[h]], -1, keepdims=True) + RMS_EPS)
          for h in heads]
    qn = [qc[:, hsl[h]] * lax.rsqrt(jnp.sum(qc[:, hsl[h]] * qc[:, hsl[h]], -1, keepdims=True) + RMS_EPS)
          * GDN_DK ** -0.5 for h in heads]
    beta = [col_of(beta_all, GDN_HEADS + h) for h in heads]
    kb = [kn[h] * beta[h] for h in heads]
    kn_c = [kn[h].astype(cdt) for h in heads]
    decay = [jnp.exp(jnp.where(tril, col_of(g_cs, h) - g_t[h:h + 1, :], -jnp.inf)) for h in heads]
    kk = [lax.dot_general(kb[h].astype(cdt), kn_c[h], _NT, preferred_element_type=F32) for h in heads]
    qk = [lax.dot_general(qn[h].astype(cdt), kn_c[h], _NT, preferred_element_type=F32) for h in heads]
    y = [-jnp.where(strict, kk[h] * decay[h], 0.0) for h in heads]
    p = [eye + y[h] for h in heads]
    for _ in range((C - 1).bit_length() - 1):
        y = [mm(y[h], y[h]) for h in heads]
        p = [p[h] + mm(p[h], y[h]) for h in heads]
    u = [mm(p[h], vc[:, hsl[h]] * beta[h]) for h in heads]
    w = [mm(p[h], kb[h] * col_of(e_cs, h)) for h in heads]
    intra = [qk[h] * decay[h] for h in heads]
    q_dec = [qn[h] * col_of(e_cs, h) for h in heads]
    kd_t = [(kn[h] * col_of(e_rem, h)).T for h in heads]
    outs = [[] for _ in heads]
    for c in range(T // C):
        rs = slice(c * C, (c + 1) * C)
        s = [st_ref[h] for h in heads]
        v_new = [u[h][rs] - mm(w[h][rs], s[h]) for h in heads]
        v_pad = [jnp.where(chunk_rows[c], jnp.concatenate([v_new[h]] * (T // C), axis=0), 0.0) for h in heads]
        for h in heads:
            outs[h].append(mm(q_dec[h][rs], s[h]) + mm(intra[h][rs], v_pad[h]))
        for h in heads:
            dec = jnp.exp(g_cs[(c + 1) * C - 1:(c + 1) * C, h:h + 1])
            st_ref[h] = s[h] * dec + mm(kd_t[h], v_pad[h])
    for h in heads:
        o = jnp.concatenate(outs[h], axis=0)
        o = o * lax.rsqrt(jnp.mean(o * o, -1, keepdims=True) + RMS_EPS) * nw_ref[...]
        zh = z_ref[:, hsl[h]]
        o_ref[:, hsl[h]] = (o * (zh * jax.nn.sigmoid(zh))).astype(o_ref.dtype)


def gdn_mixer(proj, conv_w, A_log, dt_bias, norm_w, *, batch, seq, cdt=BF16):
    M = proj.shape[0]
    T = GDN_TILE
    nt = seq // T
    W = GDN_HEADS * GDN_DK

    def seg(name):
        return pl.BlockSpec((T, SEG[name][1]),
                            functools.partial(lambda b, c, blk: (b * nt + c, blk), blk=_col_block(name)))

    full = lambda a: pl.BlockSpec(a.shape, lambda b, c: (0,) * a.ndim)
    params = [conv_w[:, :W], conv_w[:, W:2 * W], conv_w[:, 2 * W:],
              _pad_lanes(-jnp.exp(A_log.astype(F32))), _pad_lanes(dt_bias), norm_w[None, :].astype(F32)]
    return pl.pallas_call(
        functools.partial(_gdn_kernel, cdt=cdt),
        grid=(batch, nt),
        in_specs=[seg("gq"), seg("gk"), seg("gv"), seg("gz"), seg("gab")] + [full(a) for a in params],
        out_specs=pl.BlockSpec((T, GDN_HEADS * GDN_DV), lambda b, c: (b * nt + c, 0)),
        out_shape=jax.ShapeDtypeStruct((M, GDN_HEADS * GDN_DV), BF16),
        scratch_shapes=[pltpu.VMEM((CONV_TAIL, W), F32), pltpu.VMEM((CONV_TAIL, W), F32),
                        pltpu.VMEM((CONV_TAIL, W), F32),
                        pltpu.VMEM((GDN_HEADS, GDN_DK, GDN_DV), F32)],
        compiler_params=pltpu.CompilerParams(
            dimension_semantics=("parallel", "arbitrary"), vmem_limit_bytes=V7X_VMEM_LIMIT_BYTES),
        name="gdn_mixer",
    )(proj, proj, proj, proj, proj, *params)


def causal_dwconv(x, w, b=None):
    K, C = w.shape
    y = lax.conv_general_dilated(x, w[:, None, :].astype(x.dtype), window_strides=(1,),
                                 padding=[(K - 1, 0)], dimension_numbers=('NWC', 'WIO', 'NWC'),
                                 feature_group_count=C)
    return y if b is None else y + b.astype(x.dtype)


def l2norm(x):
    return x * lax.rsqrt(jnp.sum(jnp.square(x), -1, keepdims=True) + RMS_EPS)


def mamba2_ssd(z, xbc, dt_raw, conv_w, conv_b, A_log, dt_bias, D_skip, norm_w):
    Bsz, S, _ = z.shape
    G, R, P, N, Q = SSM_GROUPS, SSM_HEADS // SSM_GROUPS, SSM_HEAD_DIM, SSM_STATE, SSM_CHUNK
    nc = S // Q
    xbc = jax.nn.silu(causal_dwconv(xbc, conv_w, conv_b)).astype(F32)
    xs, Bm, Cm = jnp.split(xbc, [SSM_D_INNER, SSM_D_INNER + G * N], axis=-1)
    dt = jax.nn.softplus(dt_raw.astype(F32) + dt_bias.astype(F32))
    A = -jnp.exp(A_log.astype(F32))
    X = xs.reshape(Bsz, nc, Q, G, R, P)
    dtc = dt.reshape(Bsz, nc, Q, G, R)
    Xdt = X * dtc[..., None]
    Bc = Bm.reshape(Bsz, nc, Q, G, N)
    Cc = Cm.reshape(Bsz, nc, Q, G, N)
    a_cs = jnp.cumsum(dtc * A.reshape(G, R), axis=2)
    causal = jnp.tril(jnp.ones((Q, Q), dtype=bool))[:, :, None, None]
    seg = a_cs[:, :, :, None] - a_cs[:, :, None, :]
    Lmat = jnp.exp(jnp.where(causal, seg, -jnp.inf))
    CB = jnp.einsum('bclgn,bcsgn->bclsg', Cc, Bc)
    y_diag = jnp.einsum('bclsgr,bcsgrp->bclgrp', CB[..., None] * Lmat, Xdt)
    states = jnp.einsum('bclgn,bclgr,bclgrp->bcgrpn', Bc, jnp.exp(a_cs[:, :, -1:] - a_cs), Xdt)
    chunk_decay = jnp.exp(a_cs[:, :, -1])

    def carry_state(h, inp):
        st, dec = inp
        return h * dec[..., None, None] + st, h

    h0 = jnp.zeros((Bsz, G, R, P, N), F32)
    _, h_prev = lax.scan(carry_state, h0, (jnp.moveaxis(states, 1, 0), jnp.moveaxis(chunk_decay, 1, 0)))
    h_prev = jnp.moveaxis(h_prev, 0, 1)
    y_off = jnp.einsum('bclgn,bcgrpn,bclgr->bclgrp', Cc, h_prev, jnp.exp(a_cs))
    y = y_diag + y_off + X * D_skip.astype(F32).reshape(G, R, 1)
    y = y.reshape(Bsz, S, SSM_D_INNER) * jax.nn.silu(z.astype(F32))
    yg = y.reshape(Bsz, S, G, SSM_D_INNER // G)
    yg = yg * lax.rsqrt(jnp.mean(jnp.square(yg), -1, keepdims=True) + RMS_EPS)
    return (yg.reshape(Bsz, S, SSM_D_INNER) * norm_w).astype(z.dtype)


def gated_deltanet(qkv, z, a_raw, b_raw, conv_w, A_log, dt_bias, norm_w):
    Bsz, S, _ = qkv.shape
    H, DK, DV, C = GDN_HEADS, GDN_DK, GDN_DV, GDN_CHUNK
    nc = S // C
    qkv = jax.nn.silu(causal_dwconv(qkv, conv_w)).astype(F32)
    q, k, v = jnp.split(qkv, [H * DK, 2 * H * DK], axis=-1)
    q = l2norm(q.reshape(Bsz, S, H, DK)) * DK ** -0.5
    k = l2norm(k.reshape(Bsz, S, H, DK))
    v = v.reshape(Bsz, S, H, DV)
    beta = jax.nn.sigmoid(b_raw.astype(F32))
    g = -jnp.exp(A_log.astype(F32)) * jax.nn.softplus(a_raw.astype(F32) + dt_bias.astype(F32))

    def chunks(t):
        return jnp.moveaxis(t.reshape(Bsz, nc, C, H, *t.shape[3:]), 3, 1)

    qc, kc, vc, bc = chunks(q), chunks(k), chunks(v), chunks(beta)
    g_cs = jnp.cumsum(chunks(g), axis=-1)
    tril = jnp.tril(jnp.ones((C, C), dtype=bool))
    strict = jnp.tril(jnp.ones((C, C), dtype=bool), -1)
    decay = jnp.exp(jnp.where(tril, g_cs[..., :, None] - g_cs[..., None, :], -jnp.inf))
    kb = kc * bc[..., None]
    lower = jnp.where(strict, jnp.einsum('bhcid,bhcjd->bhcij', kb, kc) * decay, 0.0)
    eye = jnp.eye(C, dtype=F32)
    T = lax.linalg.triangular_solve(lower + eye, jnp.broadcast_to(eye, lower.shape),
                                    left_side=True, lower=True, unit_diagonal=True)
    u = jnp.einsum('bhcij,bhcje->bhcie', T, vc * bc[..., None])
    w = jnp.einsum('bhcij,bhcjd->bhcid', T, kb * jnp.exp(g_cs)[..., None])
    intra = jnp.einsum('bhcid,bhcjd->bhcij', qc, kc) * decay
    q_dec = qc * jnp.exp(g_cs)[..., None]
    k_dec = kc * jnp.exp(g_cs[..., -1:] - g_cs)[..., None]
    chunk_decay = jnp.exp(g_cs[..., -1])

    def step(state, inp):
        qd, kd, u_c, w_c, att, dec = inp
        v_new = u_c - jnp.einsum('bhid,bhde->bhie', w_c, state)
        o = jnp.einsum('bhid,bhde->bhie', qd, state) + jnp.einsum('bhij,bhje->bhie', att, v_new)
        state = state * dec[..., None, None] + jnp.einsum('bhid,bhie->bhde', kd, v_new)
        return state, o

    xs = tuple(jnp.moveaxis(t, 2, 0) for t in (q_dec, k_dec, u, w, intra, chunk_decay))
    _, o = lax.scan(step, jnp.zeros((Bsz, H, DK, DV), F32), xs)
    o = jnp.transpose(o, (1, 0, 3, 2, 4)).reshape(Bsz, S, H, DV)
    o = o * lax.rsqrt(jnp.mean(jnp.square(o), -1, keepdims=True) + RMS_EPS) * norm_w
    o = o * jax.nn.silu(z.astype(F32).reshape(Bsz, S, H, DV))
    return o.reshape(Bsz, S, H * DV).astype(z.dtype)


ATTN_TQ = 256
DENSE_TM = 1024
ROW_TILE = 256


def _seg_cols(proj, name, batch, seq):
    off, width = SEG[name]
    return proj[:, off:off + width].reshape(batch, seq, width)


def kernel(x, positions, w_in, ssm_conv_w, ssm_conv_b, ssm_A_log, ssm_dt_bias, ssm_D, ssm_norm_w,
           gdn_conv_w, gdn_A_log, gdn_dt_bias, gdn_norm_w, w_gate_up, b_gate, w_branch, w_out,
           ln1_g, ln1_b, w_ffn_in, w_ffn_out, ln2_g, ln2_b):
    B, S, D = x.shape
    M = B * S
    tables = rope_lane_tables(positions)
    topk = min(TOPK_MAX, S // 4)

    tm = min(DENSE_TM, M)
    xf = x.reshape(M, D)
    xb = xf.astype(BF16)
    for l in range(DEPTH):
        proj = matmul(xb, permute_w_in(w_in[l]), tm=tm, tn=512, out_dtype=F32)
        q, k, v, iq, ik, iwt = attn_prep(proj, tables, tr=ROW_TILE, cdt=BF16)
        y_a = dsa_attention(q, k, v, iq, ik, iwt, batch=B, seq=S, tq=ATTN_TQ, topk=topk)
        y_b = ssd_mixer(proj, ssm_conv_w[l], ssm_conv_b[l], ssm_A_log[l], ssm_dt_bias[l], ssm_D[l],
                        ssm_norm_w[l], batch=B, seq=S)
        y_c = gdn_mixer(proj, gdn_conv_w[l], gdn_A_log[l], gdn_dt_bias[l], gdn_norm_w[l], batch=B, seq=S)
        merged = branch_merge(y_a, y_b, y_c, proj, w_branch, w_gate_up, b_gate, l, tm=tm, tn=256)
        z = matmul_residual(merged, w_out, xf, alpha=DEEPNORM_ALPHA, tm=tm, tn=512, tk=D, layer=l)
        xf, xb = layer_norm(z, ln1_g[l], ln1_b[l], tr=ROW_TILE)
        h = swiglu_in(xb, w_ffn_in, l, tm=tm, tn=256)
        z = matmul_residual(h, cast_layer_bf16(w_ffn_out, l, tr=ROW_TILE), xf, alpha=DEEPNORM_ALPHA,
                            tm=tm, tn=512, tk=D_FF // 2)
        xf, xb = layer_norm(z, ln2_g[l], ln2_b[l], tr=ROW_TILE)
    return xf.reshape(B, S, D)
```

```python
import functools

import jax
import jax.numpy as jnp
import numpy as np
from jax import lax
from jax.experimental import pallas as pl
from jax.experimental.pallas import tpu as pltpu

F32 = jnp.float32
BF16 = jnp.bfloat16

D_MODEL = 4096
DEPTH = 4
HEAD_DIM = 128
ROPE_THETA = 10000.0
LN_EPS = 1e-5
RMS_EPS = 1e-6
ATT_HEADS = 12
ATT_KV_HEADS = 4
IDX_HEADS = 16
IDX_DIM = 64
TOPK_MAX = 256
SSM_D_INNER = 1536
SSM_HEAD_DIM = 64
SSM_HEADS = SSM_D_INNER // SSM_HEAD_DIM
SSM_GROUPS = 4
SSM_STATE = 128
SSM_CONV = 4
SSM_CHUNK = 128
SSM_CONV_DIM = SSM_D_INNER + 2 * SSM_GROUPS * SSM_STATE
GDN_HEADS = 12
GDN_DK = 128
GDN_DV = 128
GDN_CONV = 4
GDN_CHUNK = 64
GDN_CONV_DIM = GDN_HEADS * (2 * GDN_DK + GDN_DV)
N_BRANCH = 3
BRANCH_WIDTH = 1536
GATE_RANK = 512
D_FF = -(-8 * D_MODEL // (3 * 256)) * 256
DEEPNORM_ALPHA = (2.0 * DEPTH) ** 0.25

IN_SPLITS = (
    ATT_HEADS * HEAD_DIM, ATT_KV_HEADS * HEAD_DIM, ATT_KV_HEADS * HEAD_DIM,
    IDX_HEADS * IDX_DIM, IDX_DIM, IDX_HEADS,
    SSM_D_INNER, SSM_CONV_DIM, SSM_HEADS,
    GDN_CONV_DIM, GDN_HEADS * GDN_DV, GDN_HEADS, GDN_HEADS,
    GATE_RANK,
)
IN_WIDTH = sum(IN_SPLITS)

V7X_VMEM_LIMIT_BYTES = 56 * 1024 * 1024


def _mm_kernel(x_ref, w_ref, o_ref):
    o_ref[...] = jnp.dot(x_ref[...], w_ref[...], preferred_element_type=F32).astype(o_ref.dtype)


def matmul(x, w, *, tm, tn, out_dtype):
    M, K = x.shape
    _, N = w.shape
    return pl.pallas_call(
        _mm_kernel,
        grid=(M // tm, N // tn),
        in_specs=[pl.BlockSpec((tm, K), lambda i, j: (i, 0)),
                  pl.BlockSpec((K, tn), lambda i, j: (0, j))],
        out_specs=pl.BlockSpec((tm, tn), lambda i, j: (i, j)),
        out_shape=jax.ShapeDtypeStruct((M, N), out_dtype),
        compiler_params=pltpu.CompilerParams(
            dimension_semantics=("parallel", "arbitrary"),
            vmem_limit_bytes=V7X_VMEM_LIMIT_BYTES),
        name="matmul",
    )(x, w)


def _mm_res_kernel(x_ref, w_ref, r_ref, o_ref, acc_ref, *, alpha, nk):
    k = pl.program_id(2)

    @pl.when(k == 0)
    def _():
        acc_ref[...] = jnp.zeros_like(acc_ref)

    acc_ref[...] += jnp.dot(x_ref[...], w_ref[...].astype(x_ref.dtype), preferred_element_type=F32)

    @pl.when(k == nk - 1)
    def _():
        o_ref[...] = alpha * r_ref[...] + acc_ref[...]


def matmul_residual(x, w, res, *, alpha, tm, tn, tk, layer=None):
    M, K = x.shape
    N = w.shape[-1]
    nk = K // tk
    if layer is None:
        w_spec = pl.BlockSpec((tk, tn), lambda i, j, k: (k, j))
    else:
        w_spec = pl.BlockSpec((None, tk, tn), lambda i, j, k: (layer, k, j))
    return pl.pallas_call(
        functools.partial(_mm_res_kernel, alpha=alpha, nk=nk),
        grid=(M // tm, N // tn, nk),
        in_specs=[pl.BlockSpec((tm, tk), lambda i, j, k: (i, k)),
                  w_spec,
                  pl.BlockSpec((tm, tn), lambda i, j, k: (i, j))],
        out_specs=pl.BlockSpec((tm, tn), lambda i, j, k: (i, j)),
        out_shape=jax.ShapeDtypeStruct((M, N), F32),
        scratch_shapes=[pltpu.VMEM((tm, tn), F32)],
        compiler_params=pltpu.CompilerParams(
            dimension_semantics=("parallel", "arbitrary", "arbitrary"),
            vmem_limit_bytes=V7X_VMEM_LIMIT_BYTES),
        name="matmul_residual",
    )(x, w, res)


def _cast_kernel(w_ref, o_ref):
    o_ref[...] = w_ref[...].astype(o_ref.dtype)


def cast_layer_bf16(w, layer, *, tr):
    _, K, N = w.shape
    return pl.pallas_call(
        _cast_kernel,
        grid=(K // tr,),
        in_specs=[pl.BlockSpec((None, tr, N), lambda i: (layer, i, 0))],
        out_specs=pl.BlockSpec((tr, N), lambda i: (i, 0)),
        out_shape=jax.ShapeDtypeStruct((K, N), BF16),
        compiler_params=pltpu.CompilerParams(
            dimension_semantics=("parallel",), vmem_limit_bytes=V7X_VMEM_LIMIT_BYTES),
        name="cast_layer_bf16",
    )(w)


def _swiglu_kernel(x_ref, wg_ref, wu_ref, o_ref):
    x = x_ref[...]
    g = jnp.dot(x, wg_ref[...].astype(x.dtype), preferred_element_type=F32)
    u = jnp.dot(x, wu_ref[...].astype(x.dtype), preferred_element_type=F32)
    o_ref[...] = (g * jax.nn.sigmoid(g) * u).astype(o_ref.dtype)


def swiglu_in(x, w_in, layer, *, tm, tn):
    M, K = x.shape
    F = w_in.shape[-1] // 2
    nj = F // tn
    return pl.pallas_call(
        _swiglu_kernel,
        grid=(M // tm, nj),
        in_specs=[pl.BlockSpec((tm, K), lambda i, j: (i, 0)),
                  pl.BlockSpec((None, K, tn), lambda i, j: (layer, 0, j)),
                  pl.BlockSpec((None, K, tn), lambda i, j: (layer, 0, j + nj))],
        out_specs=pl.BlockSpec((tm, tn), lambda i, j: (i, j)),
        out_shape=jax.ShapeDtypeStruct((M, F), BF16),
        compiler_params=pltpu.CompilerParams(
            dimension_semantics=("parallel", "arbitrary"),
            vmem_limit_bytes=V7X_VMEM_LIMIT_BYTES),
        name="swiglu_in",
    )(x, w_in, w_in)


def _merge_kernel(ya_ref, yb_ref, yc_ref, gl_ref, wb_ref, wg0_ref, wg1_ref, wg2_ref, bg_ref, o_ref):
    gl = gl_ref[...].astype(BF16)
    acc = None
    for i, (y_ref, wg_ref) in enumerate(((ya_ref, wg0_ref), (yb_ref, wg1_ref), (yc_ref, wg2_ref))):
        gate = jax.nn.sigmoid(jnp.dot(gl, wg_ref[...].astype(BF16), preferred_element_type=F32) + bg_ref[i])
        term = gate * jnp.dot(y_ref[...], wb_ref[i].astype(BF16), preferred_element_type=F32)
        acc = term if acc is None else acc + term
    o_ref[...] = acc.astype(o_ref.dtype)


def branch_merge(ya, yb, yc, proj, w_branch, w_gate_up, b_gate, layer, *, tm, tn):
    M = ya.shape[0]
    L, _, _, D = w_branch.shape
    nj = D // tn
    glat_block = SEG["glat"][0] // GATE_RANK
    ysp = pl.BlockSpec((tm, BRANCH_WIDTH), lambda i, j: (i, 0))
    wg_specs = [pl.BlockSpec((None, GATE_RANK, tn),
                             functools.partial(lambda i, j, b: (layer, 0, j + b * nj), b=b))
                for b in range(N_BRANCH)]
    return pl.pallas_call(
        _merge_kernel,
        grid=(M // tm, nj),
        in_specs=[ysp, ysp, ysp,
                  pl.BlockSpec((tm, GATE_RANK), lambda i, j: (i, glat_block)),
                  pl.BlockSpec((None, N_BRANCH, BRANCH_WIDTH, tn), lambda i, j: (layer, 0, 0, j)),
                  *wg_specs,
                  pl.BlockSpec((None, N_BRANCH, 1, tn), lambda i, j: (layer, 0, 0, j))],
        out_specs=pl.BlockSpec((tm, tn), lambda i, j: (i, j)),
        out_shape=jax.ShapeDtypeStruct((M, D), BF16),
        compiler_params=pltpu.CompilerParams(
            dimension_semantics=("parallel", "arbitrary"),
            vmem_limit_bytes=V7X_VMEM_LIMIT_BYTES),
        name="branch_merge",
    )(ya, yb, yc, proj, w_branch, w_gate_up, w_gate_up, w_gate_up, b_gate.reshape(L, N_BRANCH, 1, D))


def _ln_kernel(z_ref, g_ref, b_ref, o_ref, obf_ref):
    z = z_ref[...]
    mu = jnp.mean(z, -1, keepdims=True)
    zc = z - mu
    var = jnp.mean(zc * zc, -1, keepdims=True)
    y = zc * lax.rsqrt(var + LN_EPS) * g_ref[...] + b_ref[...]
    o_ref[...] = y
    obf_ref[...] = y.astype(BF16)


def layer_norm(z, g, b, *, tr):
    M, D = z.shape
    return pl.pallas_call(
        _ln_kernel,
        grid=(M // tr,),
        in_specs=[pl.BlockSpec((tr, D), lambda i: (i, 0)),
                  pl.BlockSpec((1, D), lambda i: (0, 0)),
                  pl.BlockSpec((1, D), lambda i: (0, 0))],
        out_specs=[pl.BlockSpec((tr, D), lambda i: (i, 0)),
                   pl.BlockSpec((tr, D), lambda i: (i, 0))],
        out_shape=[jax.ShapeDtypeStruct((M, D), F32), jax.ShapeDtypeStruct((M, D), BF16)],
        compiler_params=pltpu.CompilerParams(
            dimension_semantics=("parallel",),
            vmem_limit_bytes=V7X_VMEM_LIMIT_BYTES),
        name="layer_norm",
    )(z, g.reshape(1, D), b.reshape(1, D))


def rope_tables(positions, dim):
    inv = ROPE_THETA ** (-jnp.arange(0, dim, 2, dtype=F32) / dim)
    ang = positions.astype(F32)[..., None] * inv
    return jnp.cos(ang), jnp.sin(ang)


_LAYOUT_ORDER = (
    ("aq", 1536), ("sz", 1536), ("gz", 1536), ("gq", 1536), ("gk", 1536), ("gv", 1536), ("sx", 1536),
    ("ak", 512), ("av", 512), ("sB", 512), ("sC", 512), ("glat", 512),
    ("iq", 1024), ("ik", 128), ("iw", 128), ("sdt", 128), ("gab", 128),
)


def _build_layout():
    segs, off = {}, 0
    for name, width in _LAYOUT_ORDER:
        assert off % width == 0, name
        segs[name] = (off, width)
        off += width
    return segs, off


SEG, IN_WIDTH_PADDED = _build_layout()


def _col_block(name):
    off, width = SEG[name]
    return off // width


def permute_w_in(w, dtype=BF16):
    D = w.shape[0]
    o = [int(v) for v in np.cumsum((0,) + IN_SPLITS)]
    aq, ak, av, iq, ik, iw, sz, sxbc, sdt, gqkv, gz, ga, gb, glat = [w[:, o[i]:o[i + 1]] for i in range(14)]
    half = IDX_DIM // 2
    iq_p = iq.reshape(D, IDX_HEADS // 2, 2, 2, half).transpose(0, 1, 3, 2, 4).reshape(D, IDX_HEADS * IDX_DIM)
    ik_p = jnp.repeat(ik.reshape(D, 2, 1, half), 2, axis=2).reshape(D, 2 * IDX_DIM)

    def pad128(a):
        return jnp.pad(a, ((0, 0), (0, 128 - a.shape[1])))

    G, N = SSM_GROUPS, SSM_STATE
    pieces = {
        "aq": aq, "sz": sz, "gz": gz,
        "gq": gqkv[:, :GDN_HEADS * GDN_DK], "gk": gqkv[:, GDN_HEADS * GDN_DK:2 * GDN_HEADS * GDN_DK],
        "gv": gqkv[:, 2 * GDN_HEADS * GDN_DK:],
        "sx": sxbc[:, :SSM_D_INNER], "ak": ak, "av": av,
        "sB": sxbc[:, SSM_D_INNER:SSM_D_INNER + G * N], "sC": sxbc[:, SSM_D_INNER + G * N:],
        "glat": glat, "iq": iq_p, "ik": ik_p, "iw": pad128(iw), "sdt": pad128(sdt),
        "gab": pad128(jnp.concatenate([ga, gb], axis=1)),
    }
    return jnp.concatenate([pieces[name].astype(dtype) for name, _ in _LAYOUT_ORDER], axis=1)


def rope_lane_tables(positions):
    M = positions.size
    cos, sin = rope_tables(positions.reshape(M), HEAD_DIM)
    cos_i, sin_i = rope_tables(positions.reshape(M), IDX_DIM)
    return (jnp.concatenate([cos, cos], -1), jnp.concatenate([-sin, sin], -1),
            jnp.concatenate([cos_i] * 4, -1), jnp.concatenate([-sin_i, -sin_i, sin_i, sin_i], -1))


def _attn_prep_kernel(aq_ref, ak_ref, av_ref, iq_ref, ik_ref, iw_ref, c_ref, s_ref, ci_ref, si_ref,
                      q_out, k_out, v_out, iq_out, ik_out, iwt_out):
    c, s, ci, si = c_ref[...], s_ref[...], ci_ref[...], si_ref[...]

    def rope(x, cc, ss):
        return x * cc + pltpu.roll(x, 64, 1) * ss

    for h in range(ATT_HEADS):
        sl = slice(h * HEAD_DIM, (h + 1) * HEAD_DIM)
        q_out[:, sl] = (rope(aq_ref[:, sl], c, s) * HEAD_DIM ** -0.5).astype(q_out.dtype)
    for h in range(ATT_KV_HEADS):
        sl = slice(h * HEAD_DIM, (h + 1) * HEAD_DIM)
        k_out[:, sl] = rope(ak_ref[:, sl], c, s).astype(k_out.dtype)
    v_out[...] = av_ref[...].astype(v_out.dtype)
    for p in range(IDX_HEADS // 2):
        sl = slice(p * 128, (p + 1) * 128)
        iq_out[:, sl] = (rope(iq_ref[:, sl], ci, si) * IDX_DIM ** -0.5).astype(iq_out.dtype)
    ik_out[...] = rope(ik_ref[...], ci, si).astype(ik_out.dtype)
    iwt_out[...] = (iw_ref[...] * IDX_HEADS ** -0.5).T[:IDX_HEADS, :]


def attn_prep(proj, tables, *, tr, cdt):
    M = proj.shape[0]

    def seg(name):
        return pl.BlockSpec((tr, SEG[name][1]), functools.partial(lambda i, b: (i, b), b=_col_block(name)))

    tab = pl.BlockSpec((tr, 128), lambda i: (i, 0))
    row = lambda w: pl.BlockSpec((tr, w), lambda i: (i, 0))
    return pl.pallas_call(
        _attn_prep_kernel,
        grid=(M // tr,),
        in_specs=[seg("aq"), seg("ak"), seg("av"), seg("iq"), seg("ik"), seg("iw"), tab, tab, tab, tab],
        out_specs=[row(1536), row(512), row(512), row(1024), row(128),
                   pl.BlockSpec((IDX_HEADS, tr), lambda i: (0, i))],
        out_shape=[jax.ShapeDtypeStruct((M, 1536), cdt), jax.ShapeDtypeStruct((M, 512), cdt),
                   jax.ShapeDtypeStruct((M, 512), cdt), jax.ShapeDtypeStruct((M, 1024), cdt),
                   jax.ShapeDtypeStruct((M, 128), cdt), jax.ShapeDtypeStruct((IDX_HEADS, M), F32)],
        compiler_params=pltpu.CompilerParams(
            dimension_semantics=("parallel",), vmem_limit_bytes=V7X_VMEM_LIMIT_BYTES),
        name="attn_prep",
    )(proj, proj, proj, proj, proj, proj, *tables)


MASK_BIAS = -1e30
INT16_MIN = -2 ** 15
KEY_NEG_INF = int(np.array(-np.inf, np.float32).view(np.int32)) ^ 0x7FFFFFFF
KEY_SUB = 128
SOFTMAX_ROWS = 64


def _attn_kernel(q_ref, k_ref, v_ref, iq_ref, ik_ref, iwt_ref, o_ref,
                 iq2_ref, key_ref, hi_ref, lo_ref, loeff_ref, bias_ref, j_ref, q3_ref, p_ref, m_ref, l_ref, acc_ref,
                 *, tq, topk, seq):
    qi = pl.program_id(1)
    nch = qi + 1
    n_pair = IDX_HEADS // 2
    grp = ATT_HEADS // ATT_KV_HEADS
    nt_dims = (((1,), (1,)), ((), ()))

    lane = lax.broadcasted_iota(jnp.int32, (tq, 128), 1)
    is_a = (lane & (IDX_DIM // 2)) == 0
    for p in range(n_pair):
        x = iq_ref[:, p * 128:(p + 1) * 128]
        zero = jnp.zeros_like(x)
        iq2_ref[p, :tq, :] = jnp.where(is_a, x, zero)
        iq2_ref[p, tq:, :] = jnp.where(is_a, zero, x)

    iwt = iwt_ref[...]
    q_pos = qi * tq + lax.broadcasted_iota(jnp.int32, (KEY_SUB, tq), 1)
    sub_iota = lax.broadcasted_iota(jnp.int32, (KEY_SUB, tq), 0)

    def score_chunk(c, carry):
        for sub in range(tq // KEY_SUB):
            r0 = pl.multiple_of(c * tq + sub * KEY_SUB, KEY_SUB)
            ikc = ik_ref[pl.ds(r0, KEY_SUB), :]
            acc = jnp.zeros((KEY_SUB, tq), F32)
            for p in range(n_pair):
                lt = lax.dot_general(ikc, iq2_ref[p], nt_dims, preferred_element_type=F32)
                acc = acc + iwt[2 * p:2 * p + 1, :] * jnp.maximum(lt[:, :tq], 0.0)
                acc = acc + iwt[2 * p + 1:2 * p + 2, :] * jnp.maximum(lt[:, tq:], 0.0)
            acc = jnp.where(r0 + sub_iota <= q_pos, acc, -jnp.inf)
            bits = lax.bitcast_convert_type(acc, jnp.int32)
            key = bits ^ ((bits >> 31) & 0x7FFFFFFF)
            rows = slice(sub * KEY_SUB, (sub + 1) * KEY_SUB)
            key_ref[c, rows, :] = key
            hi_ref[c, rows, :] = (key >> 16).astype(jnp.int16)
            lo_ref[c, rows, :] = ((key & 0xFFFF) + INT16_MIN).astype(jnp.int16)
        return carry

    lax.fori_loop(0, nch, score_chunk, 0)

    one = jnp.ones((tq, tq), jnp.int32)
    zero_i = jnp.zeros((tq, tq), jnp.int32)

    def count(indicator):
        def body(c, cnt8):
            ind = indicator(c, key_ref[c])
            return cnt8 + ind.reshape(4, tq // 32, 8, tq).sum(axis=1).sum(axis=0)
        cnt8 = lax.fori_loop(0, nch, body, jnp.zeros((8, tq), jnp.int32))
        return cnt8.sum(axis=0, keepdims=True)

    assert seq // 16 <= 256
    one_h = jnp.ones((tq, tq), BF16)
    zero_h = jnp.zeros((tq, tq), BF16)

    def count16(ref, cand, strict=False):
        cand16 = cand.astype(jnp.int16)

        def body(c, acc):
            half = ref[c]
            ind = jnp.where(half > cand16 if strict else half >= cand16, one_h, zero_h)
            parts = [ind[r * 16:(r + 1) * 16] for r in range(tq // 16)]
            while len(parts) > 1:
                parts = [parts[i] + parts[i + 1] for i in range(0, len(parts), 2)]
            return acc + parts[0]
        acc = lax.fori_loop(0, nch, body, jnp.zeros((16, tq), BF16))
        return acc.astype(F32).sum(axis=0, keepdims=True).astype(jnp.int32)

    def search16(ref, base, cnt_all):
        def bit_body(i, carry):
            best, cnt_best = carry
            cand = best + jnp.left_shift(jnp.int32(1), 15 - i)
            cnt = base + count16(ref, cand)
            ok = cnt >= topk
            return jnp.where(ok, cand, best), jnp.where(ok, cnt, cnt_best)
        return lax.fori_loop(0, 16, bit_body, (jnp.full((1, tq), INT16_MIN, jnp.int32), cnt_all))

    hi_thr, cnt_hi = search16(hi_ref, 0, jnp.full((1, tq), 1, jnp.int32) * (nch * tq))
    above = count16(hi_ref, hi_thr, strict=True)
    hi_thr16 = hi_thr.astype(jnp.int16)

    def mask_low(c, carry):
        loeff_ref[c] = jnp.where(hi_ref[c] == hi_thr16, lo_ref[c], jnp.int16(INT16_MIN))
        return carry

    lax.fori_loop(0, nch, mask_low, 0)
    lo_thr, cnt_lo = search16(loeff_ref, above, cnt_hi)
    lo = hi_thr * 65536 + (lo_thr - INT16_MIN)
    few = lo <= KEY_NEG_INF
    thr = jnp.where(few, KEY_NEG_INF + 1, lo)
    excess = jnp.where(few, 0, jnp.where(cnt_lo > topk, 1, 0))

    j_ref[...] = jnp.full((1, tq), seq, jnp.int32)
    key_row = lax.broadcasted_iota(jnp.int32, (tq, tq), 0)

    @pl.when(jnp.max(excess) > 0)
    def _():
        need = topk - count(lambda c, key: jnp.where(key > thr, one, zero_i))

        def idx_body(i, x):
            cand = x + jnp.left_shift(jnp.int32(1), (seq - 1).bit_length() - 1 - i)
            before = count(lambda c, key: jnp.where(
                key == thr, jnp.where(c * tq + key_row < cand, one, zero_i), zero_i))
            return jnp.where(before < need, cand, x)

        x = lax.fori_loop(0, (seq - 1).bit_length(), idx_body, jnp.zeros((1, tq), jnp.int32))
        j_ref[...] = jnp.where(excess > 0, x, seq)

    j_lim = j_ref[...]

    def bias_chunk(c, carry):
        key = key_ref[c]
        tie = jnp.where(c * tq + key_row <= j_lim, 0.0, MASK_BIAS)
        b = jnp.where(key > thr, 0.0, jnp.where(key == thr, tie, MASK_BIAS))
        bias_ref[c] = b.T
        return carry

    lax.fori_loop(0, nch, bias_chunk, 0)

    groups = range(ATT_KV_HEADS)
    hsl = [slice(g * HEAD_DIM, (g + 1) * HEAD_DIM) for g in groups]
    for g in groups:
        for r in range(grp):
            h = grp * g + r
            q3_ref[g, r * tq:(r + 1) * tq, :] = q_ref[:, h * HEAD_DIM:(h + 1) * HEAD_DIM]
    m_ref[...] = jnp.full(m_ref.shape, MASK_BIAS, F32)
    l_ref[...] = jnp.zeros(l_ref.shape, F32)
    acc_ref[...] = jnp.zeros(acc_ref.shape, F32)

    def kv_chunk(c, carry):
        r0 = pl.multiple_of(c * tq, tq)

        def scores(g):
            return lax.dot_general(q3_ref[g], k_ref[pl.ds(r0, tq), hsl[g]], nt_dims,
                                   preferred_element_type=F32)

        s_next = scores(0)
        for g in groups:
            s = s_next
            if g + 1 < ATT_KV_HEADS:
                s_next = scores(g + 1)
            vc = v_ref[pl.ds(r0, tq), hsl[g]]
            for rb in range(grp * tq // SOFTMAX_ROWS):
                rows = slice(rb * SOFTMAX_ROWS, (rb + 1) * SOFTMAX_ROWS)
                b0 = (rb * SOFTMAX_ROWS) % tq
                s_rb = s[rows] + bias_ref[c, b0:b0 + SOFTMAX_ROWS, :]
                m_prev = m_ref[g, rows, :]
                m_new = jnp.maximum(m_prev, jnp.max(s_rb, axis=-1, keepdims=True))
                alpha = jnp.exp(m_prev - m_new)
                p = jnp.exp(s_rb - jnp.concatenate([m_new] * (tq // 128), axis=1))
                l_ref[g, rows, :] = alpha * l_ref[g, rows, :] + jnp.sum(p, axis=-1, keepdims=True)
                acc_ref[g, rows, :] = alpha * acc_ref[g, rows, :]
                m_ref[g, rows, :] = m_new
                p_ref[rows, :] = p.astype(p_ref.dtype)
            acc_ref[g] += jnp.dot(p_ref[...], vc, preferred_element_type=F32)
        return carry

    lax.fori_loop(0, nch, kv_chunk, 0)
    for g in groups:
        out = acc_ref[g] / l_ref[g]
        for r in range(grp):
            h = grp * g + r
            o_ref[:, h * HEAD_DIM:(h + 1) * HEAD_DIM] = out[r * tq:(r + 1) * tq].astype(o_ref.dtype)


def dsa_attention(q, k, v, iq, ik, iwt, *, batch, seq, tq, topk):
    M = q.shape[0]
    nq = seq // tq
    grp = ATT_HEADS // ATT_KV_HEADS
    qrow = lambda w: pl.BlockSpec((tq, w), lambda b, i: (b * nq + i, 0))
    kvrow = lambda w: pl.BlockSpec((seq, w), lambda b, i: (b, 0))
    return pl.pallas_call(
        functools.partial(_attn_kernel, tq=tq, topk=topk, seq=seq),
        grid=(batch, nq),
        in_specs=[qrow(1536), kvrow(512), kvrow(512), qrow(1024), kvrow(128),
                  pl.BlockSpec((IDX_HEADS, tq), lambda b, i: (0, b * nq + i))],
        out_specs=qrow(1536),
        out_shape=jax.ShapeDtypeStruct((M, 1536), BF16),
        scratch_shapes=[
            pltpu.VMEM((IDX_HEADS // 2, 2 * tq, 128), iq.dtype),
            pltpu.VMEM((nq, tq, tq), jnp.int32),
            pltpu.VMEM((nq, tq, tq), jnp.int16),
            pltpu.VMEM((nq, tq, tq), jnp.int16),
            pltpu.VMEM((nq, tq, tq), jnp.int16),
            pltpu.VMEM((nq, tq, tq), F32),
            pltpu.VMEM((1, tq), jnp.int32),
            pltpu.VMEM((ATT_KV_HEADS, grp * tq, HEAD_DIM), q.dtype),
            pltpu.VMEM((grp * tq, tq), q.dtype),
            pltpu.VMEM((ATT_KV_HEADS, grp * tq, 128), F32),
            pltpu.VMEM((ATT_KV_HEADS, grp * tq, 128), F32),
            pltpu.VMEM((ATT_KV_HEADS, grp * tq, HEAD_DIM), F32),
        ],
        compiler_params=pltpu.CompilerParams(
            dimension_semantics=("parallel", "arbitrary"), vmem_limit_bytes=V7X_VMEM_LIMIT_BYTES),
        name="dsa_attention",
    )(q, k, v, iq, ik, iwt)


CONV_TAIL = 8


def _dwconv_silu(x, prev, w_ref, bias):
    taps = w_ref.shape[0]
    row = lax.broadcasted_iota(jnp.int32, prev.shape, 0)
    y = x * w_ref[taps - 1:taps, :]
    if bias is not None:
        y = y + bias
    for j in range(1, taps):
        xs = pltpu.roll(x, j, 0)
        head = jnp.where(row < j, pltpu.roll(prev, j, 0), xs[:CONV_TAIL])
        xs = jnp.concatenate([head, xs[CONV_TAIL:]], axis=0)
        y = y + xs * w_ref[taps - 1 - j:taps - j, :]
    return y * jax.nn.sigmoid(y)


def _softplus(x):
    return jnp.maximum(x, 0.0) + jnp.log(1.0 + jnp.exp(-jnp.abs(x)))


def _cumsum_rows(mask_bf16, v):
    hi = v.astype(BF16)
    r1 = v - hi.astype(F32)
    mid = r1.astype(BF16)
    lo = (r1 - mid.astype(F32)).astype(BF16)
    dot = lambda p: jnp.dot(mask_bf16, p, preferred_element_type=F32)
    return dot(hi) + dot(mid) + dot(lo)


_NT = (((1,), (1,)), ((), ()))


def _ssd_kernel(z_ref, x_ref, b_ref, c_ref, dt_ref, cwx_ref, cwb_ref, cwc_ref, cbx_ref, cbb_ref, cbc_ref,
                arow_ref, dtb_ref, dfull_ref, nw_ref, o_ref, tx_ref, tb_ref, tc_ref, st_ref, *, q, cdt):
    @pl.when(pl.program_id(1) == 0)
    def _():
        tx_ref[...] = jnp.zeros_like(tx_ref)
        tb_ref[...] = jnp.zeros_like(tb_ref)
        tc_ref[...] = jnp.zeros_like(tc_ref)
        st_ref[...] = jnp.zeros_like(st_ref)

    def conv(in_ref, tail_ref, w_ref, bias_ref):
        x = in_ref[...]
        y = _dwconv_silu(x, tail_ref[...], w_ref, bias_ref[...])
        tail_ref[...] = x[q - CONV_TAIL:]
        return y

    xs = conv(x_ref, tx_ref, cwx_ref, cbx_ref)
    bm = conv(b_ref, tb_ref, cwb_ref, cbb_ref)
    cm = conv(c_ref, tc_ref, cwc_ref, cbc_ref)

    P, N = SSM_HEAD_DIM, SSM_STATE
    pairs_per_group = SSM_HEADS // SSM_GROUPS // 2
    row = lax.broadcasted_iota(jnp.int32, (q, q), 0)
    col = lax.broadcasted_iota(jnp.int32, (q, q), 1)
    tril = row >= col
    lane = lax.broadcasted_iota(jnp.int32, (q, 2 * P), 1)
    lo_half = lane < P
    srow_lo = lax.broadcasted_iota(jnp.int32, (2 * P, N), 0) < P

    dt = _softplus(dt_ref[...] + dtb_ref[...])
    a_cs = _cumsum_rows(jnp.where(tril, 1.0, 0.0).astype(BF16), dt * arow_ref[...])
    a_last = a_cs[q - 1:q, :]
    e_cs = jnp.exp(a_cs)
    e_last = jnp.exp(a_last)
    w_state = dt * jnp.exp(a_last - a_cs)
    a_t = a_cs.T
    dt_t = dt.T

    ys = []
    for g in range(SSM_GROUPS):
        bg = bm[:, g * N:(g + 1) * N]
        cg = cm[:, g * N:(g + 1) * N].astype(cdt)
        cb = lax.dot_general(cg, bg.astype(cdt), _NT, preferred_element_type=F32)
        for j in range(pairs_per_group):
            t = g * pairs_per_group + j
            h0, h1 = 2 * t, 2 * t + 1
            xp = xs[:, t * 2 * P:(t + 1) * 2 * P]
            xp_c = xp.astype(cdt)
            diag = []
            for h in (h0, h1):
                seg = a_cs[:, h:h + 1] - a_t[h:h + 1, :]
                m = cb * jnp.exp(jnp.where(tril, seg, -jnp.inf)) * dt_t[h:h + 1, :]
                diag.append(jnp.dot(m.astype(cdt), xp_c, preferred_element_type=F32))
            st = st_ref[t]
            y_off = lax.dot_general(cg, st.astype(cdt), _NT, preferred_element_type=F32)
            e_pair = jnp.where(lo_half, e_cs[:, h0:h0 + 1], e_cs[:, h1:h1 + 1])
            ys.append(jnp.where(lo_half, diag[0], diag[1]) + y_off * e_pair
                      + xp * dfull_ref[:, t * 2 * P:(t + 1) * 2 * P])
            xp_t = xp.T.astype(cdt)
            z0 = jnp.dot(xp_t, (bg * w_state[:, h0:h0 + 1]).astype(cdt), preferred_element_type=F32)
            z1 = jnp.dot(xp_t, (bg * w_state[:, h1:h1 + 1]).astype(cdt), preferred_element_type=F32)
            dec = jnp.where(srow_lo, e_last[:, h0:h0 + 1], e_last[:, h1:h1 + 1])
            st_ref[t] = st * dec + jnp.where(srow_lo, z0, z1)

    gw = SSM_D_INNER // SSM_GROUPS
    tiles = gw // (2 * P)
    for g in range(SSM_GROUPS):
        yg = jnp.concatenate(ys[g * tiles:(g + 1) * tiles], axis=1)
        zg = z_ref[:, g * gw:(g + 1) * gw]
        yg = yg * (zg * jax.nn.sigmoid(zg))
        yg = yg * lax.rsqrt(jnp.mean(yg * yg, -1, keepdims=True) + RMS_EPS)
        o_ref[:, g * gw:(g + 1) * gw] = (yg * nw_ref[:, g * gw:(g + 1) * gw]).astype(o_ref.dtype)


def _pad_lanes(v, offset=0):
    return jnp.zeros((1, 128), F32).at[0, offset:offset + v.shape[0]].set(v.astype(F32))


def ssd_mixer(proj, conv_w, conv_b, A_log, dt_bias, D_skip, norm_w, *, batch, seq, cdt=BF16):
    M = proj.shape[0]
    q = SSM_CHUNK
    nc = seq // q
    GN = SSM_GROUPS * SSM_STATE
    n_pairs = SSM_HEADS // 2

    def seg(name):
        return pl.BlockSpec((q, SEG[name][1]),
                            functools.partial(lambda b, c, blk: (b * nc + c, blk), blk=_col_block(name)))

    full = lambda a: pl.BlockSpec(a.shape, lambda b, c: (0,) * a.ndim)
    params = [conv_w[:, :SSM_D_INNER], conv_w[:, SSM_D_INNER:SSM_D_INNER + GN], conv_w[:, SSM_D_INNER + GN:],
              conv_b[None, :SSM_D_INNER], conv_b[None, SSM_D_INNER:SSM_D_INNER + GN],
              conv_b[None, SSM_D_INNER + GN:],
              _pad_lanes(-jnp.exp(A_log.astype(F32))), _pad_lanes(dt_bias),
              jnp.repeat(D_skip.astype(F32), SSM_HEAD_DIM)[None, :], norm_w[None, :].astype(F32)]
    return pl.pallas_call(
        functools.partial(_ssd_kernel, q=q, cdt=cdt),
        grid=(batch, nc),
        in_specs=[seg("sz"), seg("sx"), seg("sB"), seg("sC"), seg("sdt")] + [full(a) for a in params],
        out_specs=pl.BlockSpec((q, SSM_D_INNER), lambda b, c: (b * nc + c, 0)),
        out_shape=jax.ShapeDtypeStruct((M, SSM_D_INNER), BF16),
        scratch_shapes=[pltpu.VMEM((CONV_TAIL, SSM_D_INNER), F32), pltpu.VMEM((CONV_TAIL, GN), F32),
                        pltpu.VMEM((CONV_TAIL, GN), F32),
                        pltpu.VMEM((n_pairs, 2 * SSM_HEAD_DIM, SSM_STATE), F32)],
        compiler_params=pltpu.CompilerParams(
            dimension_semantics=("parallel", "arbitrary"), vmem_limit_bytes=V7X_VMEM_LIMIT_BYTES),
        name="ssd_mixer",
    )(proj, proj, proj, proj, proj, *params)


GDN_TILE = 2 * GDN_CHUNK


def _gdn_kernel(q_ref, k_ref, v_ref, z_ref, ab_ref, cwq_ref, cwk_ref, cwv_ref, arow_ref, dtb_ref, nw_ref,
                o_ref, tq_ref, tk_ref, tv_ref, st_ref, *, cdt):
    T, C = GDN_TILE, GDN_CHUNK

    @pl.when(pl.program_id(1) == 0)
    def _():
        tq_ref[...] = jnp.zeros_like(tq_ref)
        tk_ref[...] = jnp.zeros_like(tk_ref)
        tv_ref[...] = jnp.zeros_like(tv_ref)
        st_ref[...] = jnp.zeros_like(st_ref)

    def conv(in_ref, tail_ref, w_ref):
        x = in_ref[...]
        y = _dwconv_silu(x, tail_ref[...], w_ref, None)
        tail_ref[...] = x[T - CONV_TAIL:]
        return y

    qc = conv(q_ref, tq_ref, cwq_ref)
    kc = conv(k_ref, tk_ref, cwk_ref)
    vc = conv(v_ref, tv_ref, cwv_ref)

    row = lax.broadcasted_iota(jnp.int32, (T, T), 0)
    col = lax.broadcasted_iota(jnp.int32, (T, T), 1)
    same = (row // C) == (col // C)
    tril = jnp.logical_and(same, row >= col)
    strict = jnp.logical_and(same, row > col)
    eye = jnp.where(row == col, 1.0, 0.0)
    row_w = lax.broadcasted_iota(jnp.int32, (T, 128), 0)

    ab = ab_ref[...]
    g = arow_ref[...] * _softplus(ab + dtb_ref[...])
    beta_all = jax.nn.sigmoid(ab)
    g_cs = _cumsum_rows(jnp.where(tril, 1.0, 0.0).astype(BF16), g)
    g_last = jnp.where(row_w < C, g_cs[C - 1:C, :], g_cs[T - 1:T, :])
    e_cs = jnp.exp(g_cs)
    e_rem = jnp.exp(g_last - g_cs)
    g_t = g_cs.T
    chunk_rows = [row_w < C, row_w >= C]

    def mm(a, b):
        return jnp.dot(a.astype(cdt), b.astype(cdt), preferred_element_type=F32)

    heads = range(GDN_HEADS)
    hsl = [slice(h * GDN_DK, (h + 1) * GDN_DK) for h in heads]
    col_of = lambda a, h: a[:, h:h + 1]
    kn = [kc[:, hsl[h]] * lax.rsqrt(jnp.sum(kc[:, hsl[h]] * kc[:, hsl[h]], -1, keepdims=True) + RMS_EPS)
          for h in heads]
    qn = [qc[:, hsl[h]] * lax.rsqrt(jnp.sum(qc[:, hsl[h]] * qc[:, hsl[h]], -1, keepdims=True) + RMS_EPS)
          * GDN_DK ** -0.5 for h in heads]
    beta = [col_of(beta_all, GDN_HEADS + h) for h in heads]
    kb = [kn[h] * beta[h] for h in heads]
    kn_c = [kn[h].astype(cdt) for h in heads]
    decay = [jnp.exp(jnp.where(tril, col_of(g_cs, h) - g_t[h:h + 1, :], -jnp.inf)) for h in heads]
    kk = [lax.dot_general(kb[h].astype(cdt), kn_c[h], _NT, preferred_element_type=F32) for h in heads]
    qk = [lax.dot_general(qn[h].astype(cdt), kn_c[h], _NT, preferred_element_type=F32) for h in heads]
    y = [-jnp.where(strict, kk[h] * decay[h], 0.0) for h in heads]
    p = [eye + y[h] for h in heads]
    for _ in range((C - 1).bit_length() - 1):
        y = [mm(y[h], y[h]) for h in heads]
        p = [p[h] + mm(p[h], y[h]) for h in heads]
    u = [mm(p[h], vc[:, hsl[h]] * beta[h]) for h in heads]
    w = [mm(p[h], kb[h] * col_of(e_cs, h)) for h in heads]
    intra = [qk[h] * decay[h] for h in heads]
    q_dec = [qn[h] * col_of(e_cs, h) for h in heads]
    kd_t = [(kn[h] * col_of(e_rem, h)).T for h in heads]
    outs = [[] for _ in heads]
    for c in range(T // C):
        rs = slice(c * C, (c + 1) * C)
        s = [st_ref[h] for h in heads]
        v_new = [u[h][rs] - mm(w[h][rs], s[h]) for h in heads]
        v_pad = [jnp.where(chunk_rows[c], jnp.concatenate([v_new[h]] * (T // C), axis=0), 0.0) for h in heads]
        for h in heads:
            outs[h].append(mm(q_dec[h][rs], s[h]) + mm(intra[h][rs], v_pad[h]))
        for h in heads:
            dec = jnp.exp(g_cs[(c + 1) * C - 1:(c + 1) * C, h:h + 1])
            st_ref[h] = s[h] * dec + mm(kd_t[h], v_pad[h])
    for h in heads:
        o = jnp.concatenate(outs[h], axis=0)
        o = o * lax.rsqrt(jnp.mean(o * o, -1, keepdims=True) + RMS_EPS) * nw_ref[...]
        zh = z_ref[:, hsl[h]]
        o_ref[:, hsl[h]] = (o * (zh * jax.nn.sigmoid(zh))).astype(o_ref.dtype)


def gdn_mixer(proj, conv_w, A_log, dt_bias, norm_w, *, batch, seq, cdt=BF16):
    M = proj.shape[0]
    T = GDN_TILE
    nt = seq // T
    W = GDN_HEADS * GDN_DK

    def seg(name):
        return pl.BlockSpec((T, SEG[name][1]),
                            functools.partial(lambda b, c, blk: (b * nt + c, blk), blk=_col_block(name)))

    full = lambda a: pl.BlockSpec(a.shape, lambda b, c: (0,) * a.ndim)
    params = [conv_w[:, :W], conv_w[:, W:2 * W], conv_w[:, 2 * W:],
              _pad_lanes(-jnp.exp(A_log.astype(F32))), _pad_lanes(dt_bias), norm_w[None, :].astype(F32)]
    return pl.pallas_call(
        functools.partial(_gdn_kernel, cdt=cdt),
        grid=(batch, nt),
        in_specs=[seg("gq"), seg("gk"), seg("gv"), seg("gz"), seg("gab")] + [full(a) for a in params],
        out_specs=pl.BlockSpec((T, GDN_HEADS * GDN_DV), lambda b, c: (b * nt + c, 0)),
        out_shape=jax.ShapeDtypeStruct((M, GDN_HEADS * GDN_DV), BF16),
        scratch_shapes=[pltpu.VMEM((CONV_TAIL, W), F32), pltpu.VMEM((CONV_TAIL, W), F32),
                        pltpu.VMEM((CONV_TAIL, W), F32),
                        pltpu.VMEM((GDN_HEADS, GDN_DK, GDN_DV), F32)],
        compiler_params=pltpu.CompilerParams(
            dimension_semantics=("parallel", "arbitrary"), vmem_limit_bytes=V7X_VMEM_LIMIT_BYTES),
        name="gdn_mixer",
    )(proj, proj, proj, proj, proj, *params)


ATTN_TQ = 256
DENSE_TM = 1024
ROW_TILE = 256


def kernel(x, positions, w_in, ssm_conv_w, ssm_conv_b, ssm_A_log, ssm_dt_bias, ssm_D, ssm_norm_w,
           gdn_conv_w, gdn_A_log, gdn_dt_bias, gdn_norm_w, w_gate_up, b_gate, w_branch, w_out,
           ln1_g, ln1_b, w_ffn_in, w_ffn_out, ln2_g, ln2_b):
    B, S, D = x.shape
    M = B * S
    tables = rope_lane_tables(positions)
    topk = min(TOPK_MAX, S // 4)

    tm = min(DENSE_TM, M)
    xf = x.reshape(M, D)
    xb = xf.astype(BF16)
    for l in range(DEPTH):
        proj = matmul(xb, permute_w_in(w_in[l]), tm=tm, tn=512, out_dtype=F32)
        q, k, v, iq, ik, iwt = attn_prep(proj, tables, tr=ROW_TILE, cdt=BF16)
        y_a = dsa_attention(q, k, v, iq, ik, iwt, batch=B, seq=S, tq=ATTN_TQ, topk=topk)
        y_b = ssd_mixer(proj, ssm_conv_w[l], ssm_conv_b[l], ssm_A_log[l], ssm_dt_bias[l], ssm_D[l],
                        ssm_norm_w[l], batch=B, seq=S)
        y_c = gdn_mixer(proj, gdn_conv_w[l], gdn_A_log[l], gdn_dt_bias[l], gdn_norm_w[l], batch=B, seq=S)
        merged = branch_merge(y_a, y_b, y_c, proj, w_branch, w_gate_up, b_gate, l, tm=tm, tn=256)
        z = matmul_residual(merged, w_out, xf, alpha=DEEPNORM_ALPHA, tm=tm, tn=512, tk=D, layer=l)
        xf, xb = layer_norm(z, ln1_g[l], ln1_b[l], tr=ROW_TILE)
        h = swiglu_in(xb, w_ffn_in, l, tm=tm, tn=256)
        z = matmul_residual(h, cast_layer_bf16(w_ffn_out, l, tr=ROW_TILE), xf, alpha=DEEPNORM_ALPHA,
                            tm=tm, tn=512, tk=D_FF // 2)
        xf, xb = layer_norm(z, ln2_g[l], ln2_b[l], tr=ROW_TILE)
    return xf.reshape(B, S, D)
```

```python
import functools

import jax
import jax.numpy as jnp
import numpy as np
from jax import lax
from jax.experimental import pallas as pl
from jax.experimental.pallas import tpu as pltpu

F32 = jnp.float32
BF16 = jnp.bfloat16

D_MODEL = 4096
DEPTH = 4
HEAD_DIM = 128
ROPE_THETA = 10000.0
LN_EPS = 1e-5
RMS_EPS = 1e-6
ATT_HEADS = 12
ATT_KV_HEADS = 4
IDX_HEADS = 16
IDX_DIM = 64
TOPK_MAX = 256
SSM_D_INNER = 1536
SSM_HEAD_DIM = 64
SSM_HEADS = SSM_D_INNER // SSM_HEAD_DIM
SSM_GROUPS = 4
SSM_STATE = 128
SSM_CONV = 4
SSM_CHUNK = 128
SSM_CONV_DIM = SSM_D_INNER + 2 * SSM_GROUPS * SSM_STATE
GDN_HEADS = 12
GDN_DK = 128
GDN_DV = 128
GDN_CONV = 4
GDN_CHUNK = 64
GDN_CONV_DIM = GDN_HEADS * (2 * GDN_DK + GDN_DV)
N_BRANCH = 3
BRANCH_WIDTH = 1536
GATE_RANK = 512
D_FF = -(-8 * D_MODEL // (3 * 256)) * 256
DEEPNORM_ALPHA = (2.0 * DEPTH) ** 0.25

IN_SPLITS = (
    ATT_HEADS * HEAD_DIM, ATT_KV_HEADS * HEAD_DIM, ATT_KV_HEADS * HEAD_DIM,
    IDX_HEADS * IDX_DIM, IDX_DIM, IDX_HEADS,
    SSM_D_INNER, SSM_CONV_DIM, SSM_HEADS,
    GDN_CONV_DIM, GDN_HEADS * GDN_DV, GDN_HEADS, GDN_HEADS,
    GATE_RANK,
)
IN_WIDTH = sum(IN_SPLITS)

V7X_VMEM_LIMIT_BYTES = 56 * 1024 * 1024


def _mm_kernel(x_ref, w_ref, o_ref):
    o_ref[...] = jnp.dot(x_ref[...], w_ref[...], preferred_element_type=F32).astype(o_ref.dtype)


def matmul(x, w, *, tm, tn, out_dtype):
    M, K = x.shape
    _, N = w.shape
    return pl.pallas_call(
        _mm_kernel,
        grid=(M // tm, N // tn),
        in_specs=[pl.BlockSpec((tm, K), lambda i, j: (i, 0)),
                  pl.BlockSpec((K, tn), lambda i, j: (0, j))],
        out_specs=pl.BlockSpec((tm, tn), lambda i, j: (i, j)),
        out_shape=jax.ShapeDtypeStruct((M, N), out_dtype),
        compiler_params=pltpu.CompilerParams(
            dimension_semantics=("parallel", "arbitrary"),
            vmem_limit_bytes=V7X_VMEM_LIMIT_BYTES),
        name="matmul",
    )(x, w)


def _mm_res_kernel(x_ref, w_ref, r_ref, o_ref, acc_ref, *, alpha, nk):
    k = pl.program_id(2)

    @pl.when(k == 0)
    def _():
        acc_ref[...] = jnp.zeros_like(acc_ref)

    acc_ref[...] += jnp.dot(x_ref[...], w_ref[...].astype(x_ref.dtype), preferred_element_type=F32)

    @pl.when(k == nk - 1)
    def _():
        o_ref[...] = alpha * r_ref[...] + acc_ref[...]


def matmul_residual(x, w, res, *, alpha, tm, tn, tk, layer=None):
    M, K = x.shape
    N = w.shape[-1]
    nk = K // tk
    if layer is None:
        w_spec = pl.BlockSpec((tk, tn), lambda i, j, k: (k, j))
    else:
        w_spec = pl.BlockSpec((None, tk, tn), lambda i, j, k: (layer, k, j))
    return pl.pallas_call(
        functools.partial(_mm_res_kernel, alpha=alpha, nk=nk),
        grid=(M // tm, N // tn, nk),
        in_specs=[pl.BlockSpec((tm, tk), lambda i, j, k: (i, k)),
                  w_spec,
                  pl.BlockSpec((tm, tn), lambda i, j, k: (i, j))],
        out_specs=pl.BlockSpec((tm, tn), lambda i, j, k: (i, j)),
        out_shape=jax.ShapeDtypeStruct((M, N), F32),
        scratch_shapes=[pltpu.VMEM((tm, tn), F32)],
        compiler_params=pltpu.CompilerParams(
            dimension_semantics=("parallel", "arbitrary", "arbitrary"),
            vmem_limit_bytes=V7X_VMEM_LIMIT_BYTES),
        name="matmul_residual",
    )(x, w, res)


def _cast_kernel(w_ref, o_ref):
    o_ref[...] = w_ref[...].astype(o_ref.dtype)


def cast_layer_bf16(w, layer, *, tr):
    _, K, N = w.shape
    return pl.pallas_call(
        _cast_kernel,
        grid=(K // tr,),
        in_specs=[pl.BlockSpec((None, tr, N), lambda i: (layer, i, 0))],
        out_specs=pl.BlockSpec((tr, N), lambda i: (i, 0)),
        out_shape=jax.ShapeDtypeStruct((K, N), BF16),
        compiler_params=pltpu.CompilerParams(
            dimension_semantics=("parallel",), vmem_limit_bytes=V7X_VMEM_LIMIT_BYTES),
        name="cast_layer_bf16",
    )(w)


def _swiglu_kernel(x_ref, wg_ref, wu_ref, o_ref):
    x = x_ref[...]
    g = jnp.dot(x, wg_ref[...].astype(x.dtype), preferred_element_type=F32)
    u = jnp.dot(x, wu_ref[...].astype(x.dtype), preferred_element_type=F32)
    o_ref[...] = (g * jax.nn.sigmoid(g) * u).astype(o_ref.dtype)


def swiglu_in(x, w_in, layer, *, tm, tn):
    M, K = x.shape
    F = w_in.shape[-1] // 2
    nj = F // tn
    return pl.pallas_call(
        _swiglu_kernel,
        grid=(M // tm, nj),
        in_specs=[pl.BlockSpec((tm, K), lambda i, j: (i, 0)),
                  pl.BlockSpec((None, K, tn), lambda i, j: (layer, 0, j)),
                  pl.BlockSpec((None, K, tn), lambda i, j: (layer, 0, j + nj))],
        out_specs=pl.BlockSpec((tm, tn), lambda i, j: (i, j)),
        out_shape=jax.ShapeDtypeStruct((M, F), BF16),
        compiler_params=pltpu.CompilerParams(
            dimension_semantics=("parallel", "arbitrary"),
            vmem_limit_bytes=V7X_VMEM_LIMIT_BYTES),
        name="swiglu_in",
    )(x, w_in, w_in)


def _merge_kernel(ya_ref, yb_ref, yc_ref, gl_ref, wb_ref, wg0_ref, wg1_ref, wg2_ref, bg_ref, o_ref):
    gl = gl_ref[...].astype(BF16)
    acc = None
    for i, (y_ref, wg_ref) in enumerate(((ya_ref, wg0_ref), (yb_ref, wg1_ref), (yc_ref, wg2_ref))):
        gate = jax.nn.sigmoid(jnp.dot(gl, wg_ref[...].astype(BF16), preferred_element_type=F32) + bg_ref[i])
        term = gate * jnp.dot(y_ref[...], wb_ref[i].astype(BF16), preferred_element_type=F32)
        acc = term if acc is None else acc + term
    o_ref[...] = acc.astype(o_ref.dtype)


def branch_merge(ya, yb, yc, proj, w_branch, w_gate_up, b_gate, layer, *, tm, tn):
    M = ya.shape[0]
    L, _, _, D = w_branch.shape
    nj = D // tn
    glat_block = SEG["glat"][0] // GATE_RANK
    ysp = pl.BlockSpec((tm, BRANCH_WIDTH), lambda i, j: (i, 0))
    wg_specs = [pl.BlockSpec((None, GATE_RANK, tn),
                             functools.partial(lambda i, j, b: (layer, 0, j + b * nj), b=b))
                for b in range(N_BRANCH)]
    return pl.pallas_call(
        _merge_kernel,
        grid=(M // tm, nj),
        in_specs=[ysp, ysp, ysp,
                  pl.BlockSpec((tm, GATE_RANK), lambda i, j: (i, glat_block)),
                  pl.BlockSpec((None, N_BRANCH, BRANCH_WIDTH, tn), lambda i, j: (layer, 0, 0, j)),
                  *wg_specs,
                  pl.BlockSpec((None, N_BRANCH, 1, tn), lambda i, j: (layer, 0, 0, j))],
        out_specs=pl.BlockSpec((tm, tn), lambda i, j: (i, j)),
        out_shape=jax.ShapeDtypeStruct((M, D), BF16),
        compiler_params=pltpu.CompilerParams(
            dimension_semantics=("parallel", "arbitrary"),
            vmem_limit_bytes=V7X_VMEM_LIMIT_BYTES),
        name="branch_merge",
    )(ya, yb, yc, proj, w_branch, w_gate_up, w_gate_up, w_gate_up, b_gate.reshape(L, N_BRANCH, 1, D))


def _ln_kernel(z_ref, g_ref, b_ref, o_ref, obf_ref):
    z = z_ref[...]
    mu = jnp.mean(z, -1, keepdims=True)
    zc = z - mu
    var = jnp.mean(zc * zc, -1, keepdims=True)
    y = zc * lax.rsqrt(var + LN_EPS) * g_ref[...] + b_ref[...]
    o_ref[...] = y
    obf_ref[...] = y.astype(BF16)


def layer_norm(z, g, b, *, tr):
    M, D = z.shape
    return pl.pallas_call(
        _ln_kernel,
        grid=(M // tr,),
        in_specs=[pl.BlockSpec((tr, D), lambda i: (i, 0)),
                  pl.BlockSpec((1, D), lambda i: (0, 0)),
                  pl.BlockSpec((1, D), lambda i: (0, 0))],
        out_specs=[pl.BlockSpec((tr, D), lambda i: (i, 0)),
                   pl.BlockSpec((tr, D), lambda i: (i, 0))],
        out_shape=[jax.ShapeDtypeStruct((M, D), F32), jax.ShapeDtypeStruct((M, D), BF16)],
        compiler_params=pltpu.CompilerParams(
            dimension_semantics=("parallel",),
            vmem_limit_bytes=V7X_VMEM_LIMIT_BYTES),
        name="layer_norm",
    )(z, g.reshape(1, D), b.reshape(1, D))


def rope_tables(positions, dim):
    inv = ROPE_THETA ** (-jnp.arange(0, dim, 2, dtype=F32) / dim)
    ang = positions.astype(F32)[..., None] * inv
    return jnp.cos(ang), jnp.sin(ang)


_LAYOUT_ORDER = (
    ("aq", 1536), ("sz", 1536), ("gz", 1536), ("gq", 1536), ("gk", 1536), ("gv", 1536), ("sx", 1536),
    ("ak", 512), ("av", 512), ("sB", 512), ("sC", 512), ("glat", 512),
    ("iq", 1024), ("ik", 128), ("iw", 128), ("sdt", 128), ("gab", 128),
)


def _build_layout():
    segs, off = {}, 0
    for name, width in _LAYOUT_ORDER:
        assert off % width == 0, name
        segs[name] = (off, width)
        off += width
    return segs, off


SEG, IN_WIDTH_PADDED = _build_layout()


def _col_block(name):
    off, width = SEG[name]
    return off // width


def permute_w_in(w, dtype=BF16):
    D = w.shape[0]
    o = [int(v) for v in np.cumsum((0,) + IN_SPLITS)]
    aq, ak, av, iq, ik, iw, sz, sxbc, sdt, gqkv, gz, ga, gb, glat = [w[:, o[i]:o[i + 1]] for i in range(14)]
    half = IDX_DIM // 2
    iq_p = iq.reshape(D, IDX_HEADS // 2, 2, 2, half).transpose(0, 1, 3, 2, 4).reshape(D, IDX_HEADS * IDX_DIM)
    ik_p = jnp.repeat(ik.reshape(D, 2, 1, half), 2, axis=2).reshape(D, 2 * IDX_DIM)

    def pad128(a):
        return jnp.pad(a, ((0, 0), (0, 128 - a.shape[1])))

    G, N = SSM_GROUPS, SSM_STATE
    pieces = {
        "aq": aq, "sz": sz, "gz": gz,
        "gq": gqkv[:, :GDN_HEADS * GDN_DK], "gk": gqkv[:, GDN_HEADS * GDN_DK:2 * GDN_HEADS * GDN_DK],
        "gv": gqkv[:, 2 * GDN_HEADS * GDN_DK:],
        "sx": sxbc[:, :SSM_D_INNER], "ak": ak, "av": av,
        "sB": sxbc[:, SSM_D_INNER:SSM_D_INNER + G * N], "sC": sxbc[:, SSM_D_INNER + G * N:],
        "glat": glat, "iq": iq_p, "ik": ik_p, "iw": pad128(iw), "sdt": pad128(sdt),
        "gab": pad128(jnp.concatenate([ga, gb], axis=1)),
    }
    return jnp.concatenate([pieces[name].astype(dtype) for name, _ in _LAYOUT_ORDER], axis=1)


def _relayout_kernel(w_ref, o_ref):
    w = w_ref[...]
    o = [int(v) for v in np.cumsum((0,) + IN_SPLITS)]
    src = dict(zip(("aq", "ak", "av", "iq", "ik", "iw", "sz", "sxbc", "sdt", "gqkv", "gz", "ga", "gb", "glat"), o))
    half = IDX_DIM // 2
    rows = w.shape[0]

    def cols(start, width):
        return w[:, start:start + width]

    def pad128(a):
        return jnp.concatenate([a, jnp.zeros((rows, 128 - a.shape[1]), a.dtype)], axis=1)

    GN, HK = SSM_GROUPS * SSM_STATE, GDN_HEADS * GDN_DK
    iq_tiles = []
    for p in range(IDX_HEADS // 2):
        b = src["iq"] + p * 2 * IDX_DIM
        iq_tiles += [cols(b, half), cols(b + IDX_DIM, half), cols(b + half, half), cols(b + IDX_DIM + half, half)]
    k_lo, k_hi = cols(src["ik"], half), cols(src["ik"] + half, half)
    pieces = {
        "aq": cols(src["aq"], 1536), "sz": cols(src["sz"], SSM_D_INNER), "gz": cols(src["gz"], GDN_HEADS * GDN_DV),
        "gq": cols(src["gqkv"], HK), "gk": cols(src["gqkv"] + HK, HK), "gv": cols(src["gqkv"] + 2 * HK, HK),
        "sx": cols(src["sxbc"], SSM_D_INNER), "ak": cols(src["ak"], 512), "av": cols(src["av"], 512),
        "sB": cols(src["sxbc"] + SSM_D_INNER, GN), "sC": cols(src["sxbc"] + SSM_D_INNER + GN, GN),
        "glat": cols(src["glat"], GATE_RANK),
        "iq": jnp.concatenate(iq_tiles, axis=1), "ik": jnp.concatenate([k_lo, k_lo, k_hi, k_hi], axis=1),
        "iw": pad128(cols(src["iw"], IDX_HEADS)), "sdt": pad128(cols(src["sdt"], SSM_HEADS)),
        "gab": pad128(cols(src["ga"], 2 * GDN_HEADS)),
    }
    for name, (dst, width) in SEG.items():
        o_ref[:, dst:dst + width] = pieces[name].astype(o_ref.dtype)


def relayout_w_in(w_in, layer, *, tr):
    _, D, W = w_in.shape
    return pl.pallas_call(
        _relayout_kernel,
        grid=(D // tr,),
        in_specs=[pl.BlockSpec((None, tr, W), lambda i: (layer, i, 0))],
        out_specs=pl.BlockSpec((tr, IN_WIDTH_PADDED), lambda i: (i, 0)),
        out_shape=jax.ShapeDtypeStruct((D, IN_WIDTH_PADDED), BF16),
        compiler_params=pltpu.CompilerParams(
            dimension_semantics=("parallel",), vmem_limit_bytes=V7X_VMEM_LIMIT_BYTES),
        name="relayout_w_in",
    )(w_in)


def rope_lane_tables(positions):
    M = positions.size
    cos, sin = rope_tables(positions.reshape(M), HEAD_DIM)
    cos_i, sin_i = rope_tables(positions.reshape(M), IDX_DIM)
    return (jnp.concatenate([cos, cos], -1), jnp.concatenate([-sin, sin], -1),
            jnp.concatenate([cos_i] * 4, -1), jnp.concatenate([-sin_i, -sin_i, sin_i, sin_i], -1))


def _attn_prep_kernel(aq_ref, ak_ref, av_ref, iq_ref, ik_ref, iw_ref, c_ref, s_ref, ci_ref, si_ref,
                      q_out, k_out, v_out, iq_out, ik_out, iwt_out):
    c, s, ci, si = c_ref[...], s_ref[...], ci_ref[...], si_ref[...]

    def rope(x, cc, ss):
        return x * cc + pltpu.roll(x, 64, 1) * ss

    for h in range(ATT_HEADS):
        sl = slice(h * HEAD_DIM, (h + 1) * HEAD_DIM)
        q_out[:, sl] = (rope(aq_ref[:, sl], c, s) * HEAD_DIM ** -0.5).astype(q_out.dtype)
    for h in range(ATT_KV_HEADS):
        sl = slice(h * HEAD_DIM, (h + 1) * HEAD_DIM)
        k_out[:, sl] = rope(ak_ref[:, sl], c, s).astype(k_out.dtype)
    v_out[...] = av_ref[...].astype(v_out.dtype)
    for p in range(IDX_HEADS // 2):
        sl = slice(p * 128, (p + 1) * 128)
        iq_out[:, sl] = (rope(iq_ref[:, sl], ci, si) * IDX_DIM ** -0.5).astype(iq_out.dtype)
    ik_out[...] = rope(ik_ref[...], ci, si).astype(ik_out.dtype)
    iwt_out[...] = (iw_ref[...] * IDX_HEADS ** -0.5).T[:IDX_HEADS, :]


def attn_prep(proj, tables, *, tr, cdt):
    M = proj.shape[0]

    def seg(name):
        return pl.BlockSpec((tr, SEG[name][1]), functools.partial(lambda i, b: (i, b), b=_col_block(name)))

    tab = pl.BlockSpec((tr, 128), lambda i: (i, 0))
    row = lambda w: pl.BlockSpec((tr, w), lambda i: (i, 0))
    return pl.pallas_call(
        _attn_prep_kernel,
        grid=(M // tr,),
        in_specs=[seg("aq"), seg("ak"), seg("av"), seg("iq"), seg("ik"), seg("iw"), tab, tab, tab, tab],
        out_specs=[row(1536), row(512), row(512), row(1024), row(128),
                   pl.BlockSpec((IDX_HEADS, tr), lambda i: (0, i))],
        out_shape=[jax.ShapeDtypeStruct((M, 1536), cdt), jax.ShapeDtypeStruct((M, 512), cdt),
                   jax.ShapeDtypeStruct((M, 512), cdt), jax.ShapeDtypeStruct((M, 1024), cdt),
                   jax.ShapeDtypeStruct((M, 128), cdt), jax.ShapeDtypeStruct((IDX_HEADS, M), F32)],
        compiler_params=pltpu.CompilerParams(
            dimension_semantics=("parallel",), vmem_limit_bytes=V7X_VMEM_LIMIT_BYTES),
        name="attn_prep",
    )(proj, proj, proj, proj, proj, proj, *tables)


MASK_BIAS = -1e30
INT16_MIN = -2 ** 15
KEY_NEG_INF = int(np.array(-np.inf, np.float32).view(np.int32)) ^ 0x7FFFFFFF
KEY_SUB = 128
SOFTMAX_ROWS = 64


def _attn_kernel(q_ref, k_ref, v_ref, iq_ref, ik_ref, iwt_ref, o_ref,
                 iq2_ref, key_ref, hi_ref, lo_ref, loeff_ref, bias_ref, j_ref, q3_ref, p_ref, m_ref, l_ref, acc_ref,
                 *, tq, topk, seq):
    qi = pl.program_id(1)
    nch = qi + 1
    n_pair = IDX_HEADS // 2
    grp = ATT_HEADS // ATT_KV_HEADS
    nt_dims = (((1,), (1,)), ((), ()))

    lane = lax.broadcasted_iota(jnp.int32, (tq, 128), 1)
    is_a = (lane & (IDX_DIM // 2)) == 0
    for p in range(n_pair):
        x = iq_ref[:, p * 128:(p + 1) * 128]
        zero = jnp.zeros_like(x)
        iq2_ref[p, :tq, :] = jnp.where(is_a, x, zero)
        iq2_ref[p, tq:, :] = jnp.where(is_a, zero, x)

    iwt = iwt_ref[...]
    q_pos = qi * tq + lax.broadcasted_iota(jnp.int32, (KEY_SUB, tq), 1)
    sub_iota = lax.broadcasted_iota(jnp.int32, (KEY_SUB, tq), 0)

    def score_chunk(c, carry):
        for sub in range(tq // KEY_SUB):
            r0 = pl.multiple_of(c * tq + sub * KEY_SUB, KEY_SUB)
            ikc = ik_ref[pl.ds(r0, KEY_SUB), :]
            acc = jnp.zeros((KEY_SUB, tq), F32)
            for p in range(n_pair):
                lt = lax.dot_general(ikc, iq2_ref[p], nt_dims, preferred_element_type=F32)
                acc = acc + iwt[2 * p:2 * p + 1, :] * jnp.maximum(lt[:, :tq], 0.0)
                acc = acc + iwt[2 * p + 1:2 * p + 2, :] * jnp.maximum(lt[:, tq:], 0.0)
            acc = jnp.where(r0 + sub_iota <= q_pos, acc, -jnp.inf)
            bits = lax.bitcast_convert_type(acc, jnp.int32)
            key = bits ^ ((bits >> 31) & 0x7FFFFFFF)
            rows = slice(sub * KEY_SUB, (sub + 1) * KEY_SUB)
            key_ref[c, rows, :] = key
            hi_ref[c, rows, :] = (key >> 16).astype(jnp.int16)
            lo_ref[c, rows, :] = ((key & 0xFFFF) + INT16_MIN).astype(jnp.int16)
        return carry

    lax.fori_loop(0, nch, score_chunk, 0)

    one = jnp.ones((tq, tq), jnp.int32)
    zero_i = jnp.zeros((tq, tq), jnp.int32)

    def count(indicator):
        def body(c, cnt8):
            ind = indicator(c, key_ref[c])
            return cnt8 + ind.reshape(4, tq // 32, 8, tq).sum(axis=1).sum(axis=0)
        cnt8 = lax.fori_loop(0, nch, body, jnp.zeros((8, tq), jnp.int32))
        return cnt8.sum(axis=0, keepdims=True)

    assert seq // 16 <= 256
    one_h = jnp.ones((tq, tq), BF16)
    zero_h = jnp.zeros((tq, tq), BF16)

    def count16(ref, cand, strict=False):
        cand16 = cand.astype(jnp.int16)

        def body(c, acc):
            half = ref[c]
            ind = jnp.where(half > cand16 if strict else half >= cand16, one_h, zero_h)
            parts = [ind[r * 16:(r + 1) * 16] for r in range(tq // 16)]
            while len(parts) > 1:
                parts = [parts[i] + parts[i + 1] for i in range(0, len(parts), 2)]
            return acc + parts[0]
        acc = lax.fori_loop(0, nch, body, jnp.zeros((16, tq), BF16))
        return acc.astype(F32).sum(axis=0, keepdims=True).astype(jnp.int32)

    def search16(ref, base, cnt_all):
        def bit_body(i, carry):
            best, cnt_best = carry
            cand = best + jnp.left_shift(jnp.int32(1), 15 - i)
            cnt = base + count16(ref, cand)
            ok = cnt >= topk
            return jnp.where(ok, cand, best), jnp.where(ok, cnt, cnt_best)
        return lax.fori_loop(0, 16, bit_body, (jnp.full((1, tq), INT16_MIN, jnp.int32), cnt_all))

    hi_thr, cnt_hi = search16(hi_ref, 0, jnp.full((1, tq), 1, jnp.int32) * (nch * tq))
    above = count16(hi_ref, hi_thr, strict=True)
    hi_thr16 = hi_thr.astype(jnp.int16)

    def mask_low(c, carry):
        loeff_ref[c] = jnp.where(hi_ref[c] == hi_thr16, lo_ref[c], jnp.int16(INT16_MIN))
        return carry

    lax.fori_loop(0, nch, mask_low, 0)
    lo_thr, cnt_lo = search16(loeff_ref, above, cnt_hi)
    lo = hi_thr * 65536 + (lo_thr - INT16_MIN)
    few = lo <= KEY_NEG_INF
    thr = jnp.where(few, KEY_NEG_INF + 1, lo)
    excess = jnp.where(few, 0, jnp.where(cnt_lo > topk, 1, 0))

    j_ref[...] = jnp.full((1, tq), seq, jnp.int32)
    key_row = lax.broadcasted_iota(jnp.int32, (tq, tq), 0)

    @pl.when(jnp.max(excess) > 0)
    def _():
        need = topk - count(lambda c, key: jnp.where(key > thr, one, zero_i))

        def idx_body(i, x):
            cand = x + jnp.left_shift(jnp.int32(1), (seq - 1).bit_length() - 1 - i)
            before = count(lambda c, key: jnp.where(
                key == thr, jnp.where(c * tq + key_row < cand, one, zero_i), zero_i))
            return jnp.where(before < need, cand, x)

        x = lax.fori_loop(0, (seq - 1).bit_length(), idx_body, jnp.zeros((1, tq), jnp.int32))
        j_ref[...] = jnp.where(excess > 0, x, seq)

    j_lim = j_ref[...]

    def bias_chunk(c, carry):
        key = key_ref[c]
        tie = jnp.where(c * tq + key_row <= j_lim, 0.0, MASK_BIAS)
        b = jnp.where(key > thr, 0.0, jnp.where(key == thr, tie, MASK_BIAS))
        bias_ref[c] = b.T
        return carry

    lax.fori_loop(0, nch, bias_chunk, 0)

    groups = range(ATT_KV_HEADS)
    hsl = [slice(g * HEAD_DIM, (g + 1) * HEAD_DIM) for g in groups]
    for g in groups:
        for r in range(grp):
            h = grp * g + r
            q3_ref[g, r * tq:(r + 1) * tq, :] = q_ref[:, h * HEAD_DIM:(h + 1) * HEAD_DIM]
    m_ref[...] = jnp.full(m_ref.shape, MASK_BIAS, F32)
    l_ref[...] = jnp.zeros(l_ref.shape, F32)
    acc_ref[...] = jnp.zeros(acc_ref.shape, F32)

    def kv_chunk(c, carry):
        r0 = pl.multiple_of(c * tq, tq)

        def scores(g):
            return lax.dot_general(q3_ref[g], k_ref[pl.ds(r0, tq), hsl[g]], nt_dims,
                                   preferred_element_type=F32)

        s_next = scores(0)
        for g in groups:
            s = s_next
            if g + 1 < ATT_KV_HEADS:
                s_next = scores(g + 1)
            vc = v_ref[pl.ds(r0, tq), hsl[g]]
            for rb in range(grp * tq // SOFTMAX_ROWS):
                rows = slice(rb * SOFTMAX_ROWS, (rb + 1) * SOFTMAX_ROWS)
                b0 = (rb * SOFTMAX_ROWS) % tq
                s_rb = s[rows] + bias_ref[c, b0:b0 + SOFTMAX_ROWS, :]
                m_prev = m_ref[g, rows, :]
                m_new = jnp.maximum(m_prev, jnp.max(s_rb, axis=-1, keepdims=True))
                alpha = jnp.exp(m_prev - m_new)
                p = jnp.exp(s_rb - jnp.concatenate([m_new] * (tq // 128), axis=1))
                l_ref[g, rows, :] = alpha * l_ref[g, rows, :] + jnp.sum(p, axis=-1, keepdims=True)
                acc_ref[g, rows, :] = alpha * acc_ref[g, rows, :]
                m_ref[g, rows, :] = m_new
                p_ref[rows, :] = p.astype(p_ref.dtype)
            acc_ref[g] += jnp.dot(p_ref[...], vc, preferred_element_type=F32)
        return carry

    lax.fori_loop(0, nch, kv_chunk, 0)
    for g in groups:
        out = acc_ref[g] / l_ref[g]
        for r in range(grp):
            h = grp * g + r
            o_ref[:, h * HEAD_DIM:(h + 1) * HEAD_DIM] = out[r * tq:(r + 1) * tq].astype(o_ref.dtype)


def dsa_attention(q, k, v, iq, ik, iwt, *, batch, seq, tq, topk):
    M = q.shape[0]
    nq = seq // tq
    grp = ATT_HEADS // ATT_KV_HEADS
    qrow = lambda w: pl.BlockSpec((tq, w), lambda b, i: (b * nq + i, 0))
    kvrow = lambda w: pl.BlockSpec((seq, w), lambda b, i: (b, 0))
    return pl.pallas_call(
        functools.partial(_attn_kernel, tq=tq, topk=topk, seq=seq),
        grid=(batch, nq),
        in_specs=[qrow(1536), kvrow(512), kvrow(512), qrow(1024), kvrow(128),
                  pl.BlockSpec((IDX_HEADS, tq), lambda b, i: (0, b * nq + i))],
        out_specs=qrow(1536),
        out_shape=jax.ShapeDtypeStruct((M, 1536), BF16),
        scratch_shapes=[
            pltpu.VMEM((IDX_HEADS // 2, 2 * tq, 128), iq.dtype),
            pltpu.VMEM((nq, tq, tq), jnp.int32),
            pltpu.VMEM((nq, tq, tq), jnp.int16),
            pltpu.VMEM((nq, tq, tq), jnp.int16),
            pltpu.VMEM((nq, tq, tq), jnp.int16),
            pltpu.VMEM((nq, tq, tq), F32),
            pltpu.VMEM((1, tq), jnp.int32),
            pltpu.VMEM((ATT_KV_HEADS, grp * tq, HEAD_DIM), q.dtype),
            pltpu.VMEM((grp * tq, tq), q.dtype),
            pltpu.VMEM((ATT_KV_HEADS, grp * tq, 128), F32),
            pltpu.VMEM((ATT_KV_HEADS, grp * tq, 128), F32),
            pltpu.VMEM((ATT_KV_HEADS, grp * tq, HEAD_DIM), F32),
        ],
        compiler_params=pltpu.CompilerParams(
            dimension_semantics=("parallel", "arbitrary"), vmem_limit_bytes=V7X_VMEM_LIMIT_BYTES),
        name="dsa_attention",
    )(q, k, v, iq, ik, iwt)


CONV_TAIL = 8


def _dwconv_silu(x, prev, w_ref, bias):
    taps = w_ref.shape[0]
    row = lax.broadcasted_iota(jnp.int32, prev.shape, 0)
    y = x * w_ref[taps - 1:taps, :]
    if bias is not None:
        y = y + bias
    for j in range(1, taps):
        xs = pltpu.roll(x, j, 0)
        head = jnp.where(row < j, pltpu.roll(prev, j, 0), xs[:CONV_TAIL])
        xs = jnp.concatenate([head, xs[CONV_TAIL:]], axis=0)
        y = y + xs * w_ref[taps - 1 - j:taps - j, :]
    return y * jax.nn.sigmoid(y)


def _softplus(x):
    return jnp.maximum(x, 0.0) + jnp.log(1.0 + jnp.exp(-jnp.abs(x)))


def _cumsum_rows(mask_bf16, v):
    hi = v.astype(BF16)
    r1 = v - hi.astype(F32)
    mid = r1.astype(BF16)
    lo = (r1 - mid.astype(F32)).astype(BF16)
    dot = lambda p: jnp.dot(mask_bf16, p, preferred_element_type=F32)
    return dot(hi) + dot(mid) + dot(lo)


_NT = (((1,), (1,)), ((), ()))


def _ssd_kernel(z_ref, x_ref, b_ref, c_ref, dt_ref, cwx_ref, cwb_ref, cwc_ref, cbx_ref, cbb_ref, cbc_ref,
                arow_ref, dtb_ref, dfull_ref, nw_ref, o_ref, tx_ref, tb_ref, tc_ref, st_ref, *, q, cdt):
    @pl.when(pl.program_id(1) == 0)
    def _():
        tx_ref[...] = jnp.zeros_like(tx_ref)
        tb_ref[...] = jnp.zeros_like(tb_ref)
        tc_ref[...] = jnp.zeros_like(tc_ref)
        st_ref[...] = jnp.zeros_like(st_ref)

    def conv(in_ref, tail_ref, w_ref, bias_ref):
        x = in_ref[...]
        y = _dwconv_silu(x, tail_ref[...], w_ref, bias_ref[...])
        tail_ref[...] = x[q - CONV_TAIL:]
        return y

    xs = conv(x_ref, tx_ref, cwx_ref, cbx_ref)
    bm = conv(b_ref, tb_ref, cwb_ref, cbb_ref)
    cm = conv(c_ref, tc_ref, cwc_ref, cbc_ref)

    P, N = SSM_HEAD_DIM, SSM_STATE
    pairs_per_group = SSM_HEADS // SSM_GROUPS // 2
    row = lax.broadcasted_iota(jnp.int32, (q, q), 0)
    col = lax.broadcasted_iota(jnp.int32, (q, q), 1)
    tril = row >= col
    lane = lax.broadcasted_iota(jnp.int32, (q, 2 * P), 1)
    lo_half = lane < P
    srow_lo = lax.broadcasted_iota(jnp.int32, (2 * P, N), 0) < P

    dt = _softplus(dt_ref[...] + dtb_ref[...])
    a_cs = _cumsum_rows(jnp.where(tril, 1.0, 0.0).astype(BF16), dt * arow_ref[...])
    a_last = a_cs[q - 1:q, :]
    e_cs = jnp.exp(a_cs)
    e_last = jnp.exp(a_last)
    w_state = dt * jnp.exp(a_last - a_cs)
    a_t = a_cs.T
    dt_t = dt.T

    ys = []
    for g in range(SSM_GROUPS):
        bg = bm[:, g * N:(g + 1) * N]
        cg = cm[:, g * N:(g + 1) * N].astype(cdt)
        cb = lax.dot_general(cg, bg.astype(cdt), _NT, preferred_element_type=F32)
        for j in range(pairs_per_group):
            t = g * pairs_per_group + j
            h0, h1 = 2 * t, 2 * t + 1
            xp = xs[:, t * 2 * P:(t + 1) * 2 * P]
            xp_c = xp.astype(cdt)
            diag = []
            for h in (h0, h1):
                seg = a_cs[:, h:h + 1] - a_t[h:h + 1, :]
                m = cb * jnp.exp(jnp.where(tril, seg, -jnp.inf)) * dt_t[h:h + 1, :]
                diag.append(jnp.dot(m.astype(cdt), xp_c, preferred_element_type=F32))
            st = st_ref[t]
            y_off = lax.dot_general(cg, st.astype(cdt), _NT, preferred_element_type=F32)
            e_pair = jnp.where(lo_half, e_cs[:, h0:h0 + 1], e_cs[:, h1:h1 + 1])
            ys.append(jnp.where(lo_half, diag[0], diag[1]) + y_off * e_pair
                      + xp * dfull_ref[:, t * 2 * P:(t + 1) * 2 * P])
            xp_t = xp.T.astype(cdt)
            z0 = jnp.dot(xp_t, (bg * w_state[:, h0:h0 + 1]).astype(cdt), preferred_element_type=F32)
            z1 = jnp.dot(xp_t, (bg * w_state[:, h1:h1 + 1]).astype(cdt), preferred_element_type=F32)
            dec = jnp.where(srow_lo, e_last[:, h0:h0 + 1], e_last[:, h1:h1 + 1])
            st_ref[t] = st * dec + jnp.where(srow_lo, z0, z1)

    gw = SSM_D_INNER // SSM_GROUPS
    tiles = gw // (2 * P)
    for g in range(SSM_GROUPS):
        yg = jnp.concatenate(ys[g * tiles:(g + 1) * tiles], axis=1)
        zg = z_ref[:, g * gw:(g + 1) * gw]
        yg = yg * (zg * jax.nn.sigmoid(zg))
        yg = yg * lax.rsqrt(jnp.mean(yg * yg, -1, keepdims=True) + RMS_EPS)
        o_ref[:, g * gw:(g + 1) * gw] = (yg * nw_ref[:, g * gw:(g + 1) * gw]).astype(o_ref.dtype)


def _pad_lanes(v, offset=0):
    return jnp.zeros((1, 128), F32).at[0, offset:offset + v.shape[0]].set(v.astype(F32))


def ssd_mixer(proj, conv_w, conv_b, A_log, dt_bias, D_skip, norm_w, *, batch, seq, cdt=BF16):
    M = proj.shape[0]
    q = SSM_CHUNK
    nc = seq // q
    GN = SSM_GROUPS * SSM_STATE
    n_pairs = SSM_HEADS // 2

    def seg(name):
        return pl.BlockSpec((q, SEG[name][1]),
                            functools.partial(lambda b, c, blk: (b * nc + c, blk), blk=_col_block(name)))

    full = lambda a: pl.BlockSpec(a.shape, lambda b, c: (0,) * a.ndim)
    params = [conv_w[:, :SSM_D_INNER], conv_w[:, SSM_D_INNER:SSM_D_INNER + GN], conv_w[:, SSM_D_INNER + GN:],
              conv_b[None, :SSM_D_INNER], conv_b[None, SSM_D_INNER:SSM_D_INNER + GN],
              conv_b[None, SSM_D_INNER + GN:],
              _pad_lanes(-jnp.exp(A_log.astype(F32))), _pad_lanes(dt_bias),
              jnp.repeat(D_skip.astype(F32), SSM_HEAD_DIM)[None, :], norm_w[None, :].astype(F32)]
    return pl.pallas_call(
        functools.partial(_ssd_kernel, q=q, cdt=cdt),
        grid=(batch, nc),
        in_specs=[seg("sz"), seg("sx"), seg("sB"), seg("sC"), seg("sdt")] + [full(a) for a in params],
        out_specs=pl.BlockSpec((q, SSM_D_INNER), lambda b, c: (b * nc + c, 0)),
        out_shape=jax.ShapeDtypeStruct((M, SSM_D_INNER), BF16),
        scratch_shapes=[pltpu.VMEM((CONV_TAIL, SSM_D_INNER), F32), pltpu.VMEM((CONV_TAIL, GN), F32),
                        pltpu.VMEM((CONV_TAIL, GN), F32),
                        pltpu.VMEM((n_pairs, 2 * SSM_HEAD_DIM, SSM_STATE), F32)],
        compiler_params=pltpu.CompilerParams(
            dimension_semantics=("parallel", "arbitrary"), vmem_limit_bytes=V7X_VMEM_LIMIT_BYTES),
        name="ssd_mixer",
    )(proj, proj, proj, proj, proj, *params)


GDN_TILE = 2 * GDN_CHUNK


def _gdn_kernel(q_ref, k_ref, v_ref, z_ref, ab_ref, cwq_ref, cwk_ref, cwv_ref, arow_ref, dtb_ref, nw_ref,
                o_ref, tq_ref, tk_ref, tv_ref, st_ref, *, cdt):
    T, C = GDN_TILE, GDN_CHUNK

    @pl.when(pl.program_id(1) == 0)
    def _():
        tq_ref[...] = jnp.zeros_like(tq_ref)
        tk_ref[...] = jnp.zeros_like(tk_ref)
        tv_ref[...] = jnp.zeros_like(tv_ref)
        st_ref[...] = jnp.zeros_like(st_ref)

    def conv(in_ref, tail_ref, w_ref):
        x = in_ref[...]
        y = _dwconv_silu(x, tail_ref[...], w_ref, None)
        tail_ref[...] = x[T - CONV_TAIL:]
        return y

    qc = conv(q_ref, tq_ref, cwq_ref)
    kc = conv(k_ref, tk_ref, cwk_ref)
    vc = conv(v_ref, tv_ref, cwv_ref)

    row = lax.broadcasted_iota(jnp.int32, (T, T), 0)
    col = lax.broadcasted_iota(jnp.int32, (T, T), 1)
    same = (row // C) == (col // C)
    tril = jnp.logical_and(same, row >= col)
    strict = jnp.logical_and(same, row > col)
    eye = jnp.where(row == col, 1.0, 0.0)
    row_w = lax.broadcasted_iota(jnp.int32, (T, 128), 0)

    ab = ab_ref[...]
    g = arow_ref[...] * _softplus(ab + dtb_ref[...])
    beta_all = jax.nn.sigmoid(ab)
    g_cs = _cumsum_rows(jnp.where(tril, 1.0, 0.0).astype(BF16), g)
    g_last = jnp.where(row_w < C, g_cs[C - 1:C, :], g_cs[T - 1:T, :])
    e_cs = jnp.exp(g_cs)
    e_rem = jnp.exp(g_last - g_cs)
    g_t = g_cs.T
    chunk_rows = [row_w < C, row_w >= C]

    def mm(a, b):
        return jnp.dot(a.astype(cdt), b.astype(cdt), preferred_element_type=F32)

    heads = range(GDN_HEADS)
    hsl = [slice(h * GDN_DK, (h + 1) * GDN_DK) for h in heads]
    col_of = lambda a, h: a[:, h:h + 1]
    kn = [kc[:, hsl[h]] * lax.rsqrt(jnp.sum(kc[:, hsl[h]] * kc[:, hsl[h]], -1, keepdims=True) + RMS_EPS)
          for h in heads]
    qn = [qc[:, hsl[h]] * lax.rsqrt(jnp.sum(qc[:, hsl[h]] * qc[:, hsl[h]], -1, keepdims=True) + RMS_EPS)
          * GDN_DK ** -0.5 for h in heads]
    beta = [col_of(beta_all, GDN_HEADS + h) for h in heads]
    kb = [kn[h] * beta[h] for h in heads]
    kn_c = [kn[h].astype(cdt) for h in heads]
    decay = [jnp.exp(jnp.where(tril, col_of(g_cs, h) - g_t[h:h + 1, :], -jnp.inf)) for h in heads]
    kk = [lax.dot_general(kb[h].astype(cdt), kn_c[h], _NT, preferred_element_type=F32) for h in heads]
    qk = [lax.dot_general(qn[h].astype(cdt), kn_c[h], _NT, preferred_element_type=F32) for h in heads]
    y = [-jnp.where(strict, kk[h] * decay[h], 0.0) for h in heads]
    p = [eye + y[h] for h in heads]
    for _ in range((C - 1).bit_length() - 1):
        y = [mm(y[h], y[h]) for h in heads]
        p = [p[h] + mm(p[h], y[h]) for h in heads]
    u = [mm(p[h], vc[:, hsl[h]] * beta[h]) for h in heads]
    w = [mm(p[h], kb[h] * col_of(e_cs, h)) for h in heads]
    intra = [qk[h] * decay[h] for h in heads]
    q_dec = [qn[h] * col_of(e_cs, h) for h in heads]
    kd_t = [(kn[h] * col_of(e_rem, h)).T for h in heads]
    outs = [[] for _ in heads]
    for c in range(T // C):
        rs = slice(c * C, (c + 1) * C)
        s = [st_ref[h] for h in heads]
        v_new = [u[h][rs] - mm(w[h][rs], s[h]) for h in heads]
        v_pad = [jnp.where(chunk_rows[c], jnp.concatenate([v_new[h]] * (T // C), axis=0), 0.0) for h in heads]
        for h in heads:
            outs[h].append(mm(q_dec[h][rs], s[h]) + mm(intra[h][rs], v_pad[h]))
        for h in heads:
            dec = jnp.exp(g_cs[(c + 1) * C - 1:(c + 1) * C, h:h + 1])
            st_ref[h] = s[h] * dec + mm(kd_t[h], v_pad[h])
    for h in heads:
        o = jnp.concatenate(outs[h], axis=0)
        o = o * lax.rsqrt(jnp.mean(o * o, -1, keepdims=True) + RMS_EPS) * nw_ref[...]
        zh = z_ref[:, hsl[h]]
        o_ref[:, hsl[h]] = (o * (zh * jax.nn.sigmoid(zh))).astype(o_ref.dtype)


def gdn_mixer(proj, conv_w, A_log, dt_bias, norm_w, *, batch, seq, cdt=BF16):
    M = proj.shape[0]
    T = GDN_TILE
    nt = seq // T
    W = GDN_HEADS * GDN_DK

    def seg(name):
        return pl.BlockSpec((T, SEG[name][1]),
                            functools.partial(lambda b, c, blk: (b * nt + c, blk), blk=_col_block(name)))

    full = lambda a: pl.BlockSpec(a.shape, lambda b, c: (0,) * a.ndim)
    params = [conv_w[:, :W], conv_w[:, W:2 * W], conv_w[:, 2 * W:],
              _pad_lanes(-jnp.exp(A_log.astype(F32))), _pad_lanes(dt_bias), norm_w[None, :].astype(F32)]
    return pl.pallas_call(
        functools.partial(_gdn_kernel, cdt=cdt),
        grid=(batch, nt),
        in_specs=[seg("gq"), seg("gk"), seg("gv"), seg("gz"), seg("gab")] + [full(a) for a in params],
        out_specs=pl.BlockSpec((T, GDN_HEADS * GDN_DV), lambda b, c: (b * nt + c, 0)),
        out_shape=jax.ShapeDtypeStruct((M, GDN_HEADS * GDN_DV), BF16),
        scratch_shapes=[pltpu.VMEM((CONV_TAIL, W), F32), pltpu.VMEM((CONV_TAIL, W), F32),
                        pltpu.VMEM((CONV_TAIL, W), F32),
                        pltpu.VMEM((GDN_HEADS, GDN_DK, GDN_DV), F32)],
        compiler_params=pltpu.CompilerParams(
            dimension_semantics=("parallel", "arbitrary"), vmem_limit_bytes=V7X_VMEM_LIMIT_BYTES),
        name="gdn_mixer",
    )(proj, proj, proj, proj, proj, *params)


ATTN_TQ = 256
DENSE_TM = 1024
ROW_TILE = 256
RELAYOUT_ROWS = 128


def kernel(x, positions, w_in, ssm_conv_w, ssm_conv_b, ssm_A_log, ssm_dt_bias, ssm_D, ssm_norm_w,
           gdn_conv_w, gdn_A_log, gdn_dt_bias, gdn_norm_w, w_gate_up, b_gate, w_branch, w_out,
           ln1_g, ln1_b, w_ffn_in, w_ffn_out, ln2_g, ln2_b):
    B, S, D = x.shape
    M = B * S
    tables = rope_lane_tables(positions)
    topk = min(TOPK_MAX, S // 4)

    tm = min(DENSE_TM, M)
    xf = x.reshape(M, D)
    xb = xf.astype(BF16)
    for l in range(DEPTH):
        proj = matmul(xb, relayout_w_in(w_in, l, tr=RELAYOUT_ROWS), tm=tm, tn=512, out_dtype=F32)
        q, k, v, iq, ik, iwt = attn_prep(proj, tables, tr=ROW_TILE, cdt=BF16)
        y_a = dsa_attention(q, k, v, iq, ik, iwt, batch=B, seq=S, tq=ATTN_TQ, topk=topk)
        y_b = ssd_mixer(proj, ssm_conv_w[l], ssm_conv_b[l], ssm_A_log[l], ssm_dt_bias[l], ssm_D[l],
                        ssm_norm_w[l], batch=B, seq=S)
        y_c = gdn_mixer(proj, gdn_conv_w[l], gdn_A_log[l], gdn_dt_bias[l], gdn_norm_w[l], batch=B, seq=S)
        merged = branch_merge(y_a, y_b, y_c, proj, w_branch, w_gate_up, b_gate, l, tm=tm, tn=256)
        z = matmul_residual(merged, w_out, xf, alpha=DEEPNORM_ALPHA, tm=tm, tn=512, tk=D, layer=l)
        xf, xb = layer_norm(z, ln1_g[l], ln1_b[l], tr=ROW_TILE)
        h = swiglu_in(xb, w_ffn_in, l, tm=tm, tn=256)
        z = matmul_residual(h, cast_layer_bf16(w_ffn_out, l, tr=ROW_TILE), xf, alpha=DEEPNORM_ALPHA,
                            tm=tm, tn=512, tk=D_FF // 2)
        xf, xb = layer_norm(z, ln2_g[l], ln2_b[l], tr=ROW_TILE)
    return xf.reshape(B, S, D)
```
